```python
import jax
import jax.numpy as jnp
from jax import lax
import numpy as np

D_MODEL = 1024
BATCH = 2
SEQ = 8192
DEPTH = 2

CTX_LEN = 256
GRID_W = 64
NORM_EPS = 1e-6
N_BRANCHES = 3

GLA_HEADS = 4
GLA_DK = D_MODEL // (2 * GLA_HEADS)
GLA_DV = D_MODEL // GLA_HEADS
GLA_QK = GLA_HEADS * GLA_DK
GLA_V = GLA_HEADS * GLA_DV
GLA_RANK = 16
GLA_TEMP = 16.0
GLA_CHUNK = 64

HEAD_DIM = 128
ATT_Q_HEADS = D_MODEL // HEAD_DIM
ATT_KV_HEADS = ATT_Q_HEADS // 4
ATT_GROUP = ATT_Q_HEADS // ATT_KV_HEADS
ATT_Q = ATT_Q_HEADS * HEAD_DIM
ATT_KV = ATT_KV_HEADS * HEAD_DIM
ROPE_THETA = 10000.0
Q_BLOCK = 128

LRU_WIDTH = D_MODEL
LRU_BLOCKS = 8
LRU_BW = LRU_WIDTH // LRU_BLOCKS
LRU_C = 8.0
LRU_CONV = 4
LRU_CONV_LEFT = 2

D_FF = 2816
FFN_CONV = 3
FFN_CONV_LEFT = 1

IN_SPLITS = (GLA_QK, GLA_QK, GLA_V, GLA_V, GLA_RANK, GLA_RANK,
             ATT_Q, ATT_KV, ATT_KV, LRU_WIDTH, LRU_WIDTH, N_BRANCHES * D_MODEL)
D_IN = 2 * GLA_QK + 2 * GLA_V + 2 * GLA_RANK + ATT_Q + 2 * ATT_KV + 2 * LRU_WIDTH + N_BRANCHES * D_MODEL

kernel_name = "hybrid_gla_gqa_rglru_flow_block"


def rms_norm(x, g):
    xf = x.astype(jnp.float32)
    y = xf * lax.rsqrt(jnp.mean(jnp.square(xf), axis=-1, keepdims=True) + NORM_EPS)
    return (y * g.astype(jnp.float32)).astype(x.dtype)


def modulate(x, shift, scale):
    return x * (1 + scale) + shift


def depthwise_conv(x, w, b, left):
    k = w.shape[0]
    y = lax.conv_general_dilated(
        x, w[:, None, :].astype(x.dtype), window_strides=(1,), padding=[(left, k - 1 - left)],
        dimension_numbers=('NWC', 'WIO', 'NWC'), feature_group_count=x.shape[-1])
    return y + b.astype(x.dtype)


def split_proj(p):
    idx = np.cumsum(IN_SPLITS)[:-1].tolist()
    return jnp.split(p, idx, axis=-1)


def axial_rope_tables(rows):
    row = jnp.repeat(jnp.arange(rows), GRID_W)
    col = jnp.tile(jnp.arange(GRID_W), rows)
    axis_dim = HEAD_DIM // 2
    inv_freq = 1.0 / (ROPE_THETA ** (jnp.arange(0, axis_dim, 2, dtype=jnp.float32) / axis_dim))
    ang = jnp.stack([row, col], axis=-1).astype(jnp.float32)[:, :, None] * inv_freq
    return jnp.cos(ang), jnp.sin(ang)


def apply_rope(x, cos, sin):
    shp = x.shape
    xr = x.astype(jnp.float32).reshape(shp[:-1] + (2, 2, HEAD_DIM // 4))
    x1, x2 = xr[..., 0, :], xr[..., 1, :]
    out = jnp.stack([x1 * cos - x2 * sin, x1 * sin + x2 * cos], axis=-2)
    return out.reshape(shp).astype(x.dtype)


def gla_chunked(q, k, v, log_a, s0):
    b_, h_, n_, _ = q.shape
    dv = v.shape[-1]
    nc = n_ // GLA_CHUNK

    def chunks(t):
        return jnp.moveaxis(t.reshape(b_, h_, nc, GLA_CHUNK, t.shape[-1]), 2, 0)

    q, k, v, log_a = chunks(q), chunks(k), chunks(v), chunks(log_a)
    cum = jnp.cumsum(log_a, axis=-2)
    total = cum[..., -1:, :]
    q_dec = q * jnp.exp(cum)
    k_inv = k * jnp.exp(-cum)
    k_end = k * jnp.exp(total - cum)
    lower_tri = jnp.tril(jnp.ones((GLA_CHUNK, GLA_CHUNK), dtype=bool))

    def step(s, inp):
        qd, ki, ke, vc, tot = inp
        att = jnp.where(lower_tri, jnp.einsum('bhtd,bhsd->bhts', qd, ki), 0.0)
        o = jnp.einsum('bhts,bhsv->bhtv', att, vc) + jnp.einsum('bhtd,bhdv->bhtv', qd, s)
        s = jnp.exp(tot[..., 0, :])[..., None] * s + jnp.einsum('bhsd,bhsv->bhdv', ke, vc)
        return s, o

    s_last, o = lax.scan(step, s0, (q_dec, k_inv, k_end, v, total))
    o = jnp.moveaxis(o, 0, 2).reshape(b_, h_, n_, dv)
    return o, s_last


def gla_output(o, r, g_norm):
    b, h, n, dv = o.shape
    o = rms_norm(jnp.swapaxes(o, 1, 2), g_norm.reshape(h, dv)).reshape(b, n, h * dv)
    return o.astype(r.dtype) * jax.nn.silu(r)


def gla_branch(pc, pl, w_dec, b_dec, g_norm, need_ctx):
    def prep(p):
        q, k, v, r, lr_f, lr_b = p
        b, n, _ = q.shape

        def heads(t, d):
            return jnp.swapaxes(t.reshape(b, n, GLA_HEADS, d), 1, 2).astype(jnp.float32)

        log_a = [heads(jax.nn.log_sigmoid((lr @ w_dec[d] + b_dec[d]).astype(jnp.float32)) / GLA_TEMP, GLA_DK)
                 for d, lr in enumerate((lr_f, lr_b))]
        return heads(q, GLA_DK) * GLA_DK ** -0.5, heads(k, GLA_DK), heads(v, GLA_DV), log_a, r

    qc, kc, vc, lac, rc = prep(pc)
    ql, kl, vl, lal, rl = prep(pl)
    s0 = jnp.zeros(qc.shape[:2] + (GLA_DK, GLA_DV), jnp.float32)
    o_lat, o_ctx = 0.0, 0.0
    for d in range(2):
        f = (lambda t: t) if d == 0 else (lambda t: jnp.flip(t, axis=2))
        oc, sc = gla_chunked(f(qc), f(kc), f(vc), f(lac[d]), s0)
        ol, _ = gla_chunked(f(ql), f(kl), f(vl), f(lal[d]), sc)
        o_lat = o_lat + f(ol)
        o_ctx = o_ctx + f(oc)
    out_l = gla_output(o_lat, rl, g_norm)
    out_c = gla_output(o_ctx, rc, g_norm) if need_ctx else None
    return out_l, out_c


def gqa_attend(q, k, v):
    s = jnp.einsum('bkgqd,bksd->bkgqs', q, k, preferred_element_type=jnp.float32) * HEAD_DIM ** -0.5
    p = jax.nn.softmax(s, axis=-1)
    return jnp.einsum('bkgqs,bksd->bkgqd', p.astype(v.dtype), v)


def merge_heads(o):
    b, h, n, d = o.shape
    return jnp.swapaxes(o, 1, 2).reshape(b, n, h * d)


def attention_branch(pc, pl, g_q, g_k, cos, sin, need_ctx):
    def heads(t, h):
        b, n, _ = t.shape
        return jnp.swapaxes(t.reshape(b, n, h, HEAD_DIM), 1, 2)

    def prep(p):
        q, k, v = p
        return (rms_norm(heads(q, ATT_Q_HEADS), g_q), rms_norm(heads(k, ATT_KV_HEADS), g_k),
                heads(v, ATT_KV_HEADS))

    qc, kc, vc = prep(pc)
    ql, kl, vl = prep(pl)
    ql = apply_rope(ql, cos, sin)
    kl = apply_rope(kl, cos, sin)
    k_all = jnp.concatenate([kc, kl], axis=2)
    v_all = jnp.concatenate([vc, vl], axis=2)
    b, _, n, _ = ql.shape
    qb = ql.reshape(b, ATT_KV_HEADS, ATT_GROUP, n // Q_BLOCK, Q_BLOCK, HEAD_DIM)
    qb = jnp.moveaxis(qb, 3, 0)
    ol = lax.map(lambda q: gqa_attend(q, k_all, v_all), qb)
    ol = jnp.moveaxis(ol, 0, 3).reshape(b, ATT_Q_HEADS, n, HEAD_DIM)
    out_l = merge_heads(ol)
    out_c = None
    if need_ctx:
        nc = qc.shape[2]
        oc = gqa_attend(qc.reshape(b, ATT_KV_HEADS, ATT_GROUP, nc, HEAD_DIM), kc, vc)
        out_c = merge_heads(oc.reshape(b, ATT_Q_HEADS, nc, HEAD_DIM))
    return out_l, out_c


def lru_coeffs(x, w_gates, b_gates, lam):
    b, n, w = x.shape
    xb = x.reshape(b, n, LRU_BLOCKS, LRU_BW)
    z = jnp.einsum('bnhi,ghij->gbnhj', xb, w_gates).reshape(2, b, n, w) + b_gates[:, None, None, :]
    gate_r = jax.nn.sigmoid(z[0].astype(jnp.float32))
    gate_i = jax.nn.sigmoid(z[1].astype(jnp.float32))
    log_a = -LRU_C * gate_r * jax.nn.softplus(-lam.astype(jnp.float32))
    u = jnp.sqrt(-jnp.expm1(2.0 * log_a)) * (gate_i * x.astype(jnp.float32))
    return jnp.exp(log_a), u


def lru_scan(a, u, h0):
    def step(h, au):
        h = au[0] * h + au[1]
        return h, h

    h_last, hs = lax.scan(step, h0, (jnp.swapaxes(a, 0, 1), jnp.swapaxes(u, 0, 1)))
    return jnp.swapaxes(hs, 0, 1), h_last


def rglru_branch(pc, pl, conv_w, conv_b, w_gates, b_gates, lam, need_ctx):
    xc = depthwise_conv(pc[0], conv_w, conv_b, LRU_CONV_LEFT)
    xl = depthwise_conv(pl[0], conv_w, conv_b, LRU_CONV_LEFT)
    h0 = jnp.zeros((xc.shape[0], LRU_WIDTH), jnp.float32)
    h_lat, h_ctx = 0.0, 0.0
    for d in range(2):
        f = (lambda t: t) if d == 0 else (lambda t: jnp.flip(t, axis=1))
        hc, hc_last = lru_scan(*lru_coeffs(f(xc), w_gates[d], b_gates[d], lam[d]), h0)
        hl, _ = lru_scan(*lru_coeffs(f(xl), w_gates[d], b_gates[d], lam[d]), hc_last)
        h_lat = h_lat + f(hl)
        h_ctx = h_ctx + f(hc)
    yl = pl[1]
    out_l = (h_lat * jax.nn.gelu(yl.astype(jnp.float32))).astype(yl.dtype)
    out_c = None
    if need_ctx:
        yc = pc[1]
        out_c = (h_ctx * jax.nn.gelu(yc.astype(jnp.float32))).astype(yc.dtype)
    return out_l, out_c


def merge_branches(branches, gate_logits, w_branch, w_out):
    gates = jnp.split(jax.nn.sigmoid(gate_logits.astype(jnp.float32)).astype(gate_logits.dtype), N_BRANCHES, axis=-1)
    mixed = gates[0] * (branches[0] @ w_branch[0])
    for i in range(1, N_BRANCHES):
        mixed = mixed + gates[i] * (branches[i] @ w_branch[i])
    return mixed @ w_out


def conv_ffn(h, w_up, conv_w, conv_b, w_down):
    u = depthwise_conv(h @ w_up, conv_w, conv_b, FFN_CONV_LEFT)
    g, v = jnp.split(u, 2, axis=-1)
    return (jax.nn.silu(g) * v) @ w_down


def setup_inputs(seed: int = 0) -> dict:
    key = jax.random.key(seed)
    ks = jax.random.split(key, 24)
    f32 = jnp.float32

    def nrm(k, shape, fan_in, gain=1.0):
        return jax.random.normal(k, shape, f32) * (gain * fan_in ** -0.5)

    def small(k, shape, s=0.02):
        return s * jax.random.normal(k, shape, f32)

    u = jax.random.uniform(ks[17], (DEPTH, 2, LRU_WIDTH), f32, 0.9, 0.999)
    s = u ** (1.0 / LRU_C)
    return {
        'x': jax.random.normal(ks[0], (BATCH, SEQ, D_MODEL), f32),
        'c': jax.random.normal(ks[1], (BATCH, D_MODEL), f32),
        'ctx': jax.random.normal(ks[2], (BATCH, CTX_LEN, D_MODEL), f32),
        'c_ctx': jax.random.normal(ks[3], (D_MODEL,), f32),
        'w_ada': nrm(ks[4], (DEPTH, D_MODEL, 6 * D_MODEL), D_MODEL, 0.5),
        'b_ada': small(ks[5], (DEPTH, 6 * D_MODEL)),
        'norm_gains': 1.0 + small(ks[6], (DEPTH, 4, D_MODEL), 0.05),
        'w_in': nrm(ks[7], (DEPTH, D_MODEL, D_IN), D_MODEL),
        'gla_w_decay': nrm(ks[8], (DEPTH, 2, GLA_RANK, GLA_QK), GLA_RANK),
        'gla_b_decay': small(ks[9], (DEPTH, 2, GLA_QK), 0.1),
        'gla_norm_g': 1.0 + small(ks[10], (DEPTH, GLA_V), 0.05),
        'att_q_norm_g': 1.0 + small(ks[11], (DEPTH, HEAD_DIM), 0.05),
        'att_k_norm_g': 1.0 + small(ks[12], (DEPTH, HEAD_DIM), 0.05),
        'lru_conv_w': nrm(ks[13], (DEPTH, LRU_CONV, LRU_WIDTH), LRU_CONV),
        'lru_conv_b': small(ks[14], (DEPTH, LRU_WIDTH)),
        'lru_w_gates': nrm(ks[15], (DEPTH, 2, 2, LRU_BLOCKS, LRU_BW, LRU_BW), LRU_BW),
        'lru_b_gates': small(ks[16], (DEPTH, 2, 2, LRU_WIDTH)),
        'lru_lambda': jnp.log(s) - jnp.log1p(-s),
        'w_branch': nrm(ks[18], (DEPTH, N_BRANCHES, D_MODEL, D_MODEL), D_MODEL),
        'w_out': nrm(ks[19], (DEPTH, D_MODEL, D_MODEL), D_MODEL),
        'ffn_w_up': nrm(ks[20], (DEPTH, D_MODEL, 2 * D_FF), D_MODEL),
        'ffn_conv_w': nrm(ks[21], (DEPTH, FFN_CONV, 2 * D_FF), FFN_CONV),
        'ffn_conv_b': small(ks[22], (DEPTH, 2 * D_FF)),
        'ffn_w_down': nrm(ks[23], (DEPTH, D_FF, D_MODEL), D_FF),
    }


def reference(x, c, ctx, c_ctx, w_ada, b_ada, norm_gains, w_in, gla_w_decay, gla_b_decay, gla_norm_g,
              att_q_norm_g, att_k_norm_g, lru_conv_w, lru_conv_b, lru_w_gates, lru_b_gates, lru_lambda,
              w_branch, w_out, ffn_w_up, ffn_conv_w, ffn_conv_b, ffn_w_down):
    rows = x.shape[1] // GRID_W
    cos, sin = axial_rope_tables(rows)
    for l in range(DEPTH):
        need_ctx = l < DEPTH - 1
        mod_l = jnp.split((jax.nn.silu(c) @ w_ada[l] + b_ada[l])[:, None, :], 6, axis=-1)
        mod_c = jnp.split(jax.nn.silu(c_ctx) @ w_ada[l] + b_ada[l], 6, axis=-1)
        g_pre_mix, g_post_mix, g_pre_ffn, g_post_ffn = norm_gains[l]

        h_l = modulate(rms_norm(x, g_pre_mix), mod_l[0], mod_l[1])
        h_c = modulate(rms_norm(ctx, g_pre_mix), mod_c[0], mod_c[1])
        p_l = split_proj(h_l @ w_in[l])
        p_c = split_proj(h_c @ w_in[l])
        gla_l, gla_c = gla_branch(p_c[0:6], p_l[0:6], gla_w_decay[l], gla_b_decay[l], gla_norm_g[l], need_ctx)
        att_l, att_c = attention_branch(p_c[6:9], p_l[6:9], att_q_norm_g[l], att_k_norm_g[l], cos, sin, need_ctx)
        lru_l, lru_c = rglru_branch(p_c[9:11], p_l[9:11], lru_conv_w[l], lru_conv_b[l], lru_w_gates[l],
                                    lru_b_gates[l], lru_lambda[l], need_ctx)
        y_l = merge_branches((gla_l, att_l, lru_l), p_l[11], w_branch[l], w_out[l])
        x = x + mod_l[2] * rms_norm(y_l, g_post_mix)
        if need_ctx:
            y_c = merge_branches((gla_c, att_c, lru_c), p_c[11], w_branch[l], w_out[l])
            ctx = ctx + mod_c[2] * rms_norm(y_c, g_post_mix)

        h_l = modulate(rms_norm(x, g_pre_ffn), mod_l[3], mod_l[4])
        x = x + mod_l[5] * rms_norm(conv_ffn(h_l, ffn_w_up[l], ffn_conv_w[l], ffn_conv_b[l], ffn_w_down[l]), g_post_ffn)
        if need_ctx:
            h_c = modulate(rms_norm(ctx, g_pre_ffn), mod_c[3], mod_c[4])
            ctx = ctx + mod_c[5] * rms_norm(conv_ffn(h_c, ffn_w_up[l], ffn_conv_w[l], ffn_conv_b[l], ffn_w_down[l]), g_post_ffn)
    return x
```

```python
import functools

import jax
import jax.numpy as jnp
from jax import lax
from jax.experimental import pallas as pl
from jax.experimental.pallas import tpu as pltpu

F32 = jnp.float32
BF16 = jnp.bfloat16

D_MODEL = 1024
NORM_EPS = 1e-6
N_BRANCHES = 3
GRID_W = 64

GLA_HEADS = 4
GLA_DK = D_MODEL // (2 * GLA_HEADS)
GLA_DV = D_MODEL // GLA_HEADS
GLA_QK = GLA_HEADS * GLA_DK
GLA_V = GLA_HEADS * GLA_DV
GLA_RANK = 16
GLA_TEMP = 16.0
GLA_CHUNK = 64

HEAD_DIM = 128
ATT_Q_HEADS = D_MODEL // HEAD_DIM
ATT_KV_HEADS = ATT_Q_HEADS // 4
ATT_GROUP = ATT_Q_HEADS // ATT_KV_HEADS
ATT_Q = ATT_Q_HEADS * HEAD_DIM
ATT_KV = ATT_KV_HEADS * HEAD_DIM
ROPE_THETA = 10000.0

LRU_WIDTH = D_MODEL
LRU_BLOCKS = 8
LRU_BW = LRU_WIDTH // LRU_BLOCKS
LRU_C = 8.0
LRU_CONV = 4
LRU_CONV_LEFT = 2

D_FF = 2816
FFN_CONV = 3

IN_SPLITS = (GLA_QK, GLA_QK, GLA_V, GLA_V, GLA_RANK, GLA_RANK,
             ATT_Q, ATT_KV, ATT_KV, LRU_WIDTH, LRU_WIDTH, N_BRANCHES * D_MODEL)

SUBLANES = 8
LANES = 128
TIME_BLOCK = 256

COL_GQ = 0
COL_GK = 512
COL_GV = 1024
COL_GR = 2048
COL_AQ = 3072
COL_LX = 4096
COL_LY = 5120
COL_GATE = 6144
COL_AK = 9216
COL_AV = 9472
COL_LR = 9728
D_IN_PAD = 10240

VMEM_LIMIT = 56 * 1024 * 1024


def _cparams(sem):
    return pltpu.CompilerParams(dimension_semantics=sem, vmem_limit_bytes=VMEM_LIMIT)


def _sigmoid(x):
    return 1.0 / (1.0 + jnp.exp(-x))


def _silu(x):
    return x * _sigmoid(x)


def _gelu_tanh(x):
    return x * (0.5 * (1.0 + jnp.tanh(0.7978845608028654 * (x + 0.044715 * (x * x * x)))))


def _rms(x, g):
    ms = jnp.mean(x * x, axis=-1, keepdims=True)
    return x * lax.rsqrt(ms + NORM_EPS) * g


def _ada_kernel(c_ref, w_ref, b_ref, o_ref):
    c = c_ref[...]
    o_ref[0] = jnp.dot(_silu(c), w_ref[0], preferred_element_type=F32,
                       precision=lax.Precision.HIGHEST) + b_ref[0]


def _ada_call(cvec, w_ada, b_ada):
    depth, d, n = w_ada.shape
    tn = 1536
    rows = cvec.shape[0]
    return pl.pallas_call(
        _ada_kernel,
        grid=(depth, n // tn),
        in_specs=[pl.BlockSpec((rows, d), lambda l, j: (0, 0)),
                  pl.BlockSpec((1, d, tn), lambda l, j: (l, 0, j)),
                  pl.BlockSpec((1, 1, tn), lambda l, j: (l, 0, j))],
        out_specs=pl.BlockSpec((1, rows, tn), lambda l, j: (l, 0, j)),
        out_shape=jax.ShapeDtypeStruct((depth, rows, n), F32),
        compiler_params=_cparams(("parallel", "parallel")),
        name="ada_mod",
    )(cvec, w_ada, b_ada.reshape(depth, 1, n))


def _nmm_kernel(x_ref, g_ref, mod_ref, w_ref, o_ref, h_ref, *, shift_idx, scale_idx, tm, nctx, tiles_per_batch):
    i = pl.program_id(0)

    @pl.when(pl.program_id(1) == 0)
    def _():
        y = _rms(x_ref[...], g_ref[...])
        row = (i % tiles_per_batch) * tm + lax.broadcasted_iota(jnp.int32, (tm, 1), 0)
        is_ctx = row < nctx
        shift = jnp.where(is_ctx, mod_ref[0, shift_idx:shift_idx + 1, :], mod_ref[0, 6 + shift_idx:7 + shift_idx, :])
        scale = jnp.where(is_ctx, mod_ref[0, scale_idx:scale_idx + 1, :], mod_ref[0, 6 + scale_idx:7 + scale_idx, :])
        h_ref[...] = (y * (1.0 + scale) + shift).astype(BF16)

    o_ref[...] = jnp.dot(h_ref[...], w_ref[...], preferred_element_type=F32).astype(o_ref.dtype)


def _nmm_call(x, g, modsel, w, *, shift_idx, scale_idx, tm, tn, t_len, nctx, name):
    m, d = x.shape
    n = w.shape[1]
    tiles_per_batch = t_len // tm
    kern = functools.partial(_nmm_kernel, shift_idx=shift_idx, scale_idx=scale_idx, tm=tm, nctx=nctx,
                             tiles_per_batch=tiles_per_batch)
    return pl.pallas_call(
        kern,
        grid=(m // tm, n // tn),
        in_specs=[pl.BlockSpec((tm, d), lambda i, j: (i, 0)),
                  pl.BlockSpec((1, d), lambda i, j: (0, 0)),
                  pl.BlockSpec((1, 12, d), lambda i, j: (i // tiles_per_batch, 0, 0)),
                  pl.BlockSpec((d, tn), lambda i, j: (0, j))],
        out_specs=pl.BlockSpec((tm, tn), lambda i, j: (i, j)),
        out_shape=jax.ShapeDtypeStruct((m, n), BF16),
        scratch_shapes=[pltpu.VMEM((tm, d), BF16)],
        compiler_params=_cparams(("parallel", "arbitrary")),
        name=name,
    )(x, g, modsel, w)


def _log_sigmoid(z):
    return jnp.minimum(z, 0.0) - jnp.log(1.0 + jnp.exp(-jnp.abs(z)))


def _gla_kernel(q_ref, k_ref, v_ref, lr_ref, wd_ref, bd_ref, o_ref, st_ref, *, reverse, tb):
    c = GLA_CHUNK
    nchunk = tb // c

    @pl.when(pl.program_id(1) == 0)
    def _():
        st_ref[...] = jnp.zeros_like(st_ref)

    z = jnp.dot(lr_ref[...].astype(F32), wd_ref[0], preferred_element_type=F32,
                precision=lax.Precision.HIGHEST) + bd_ref[0]
    log_a = _log_sigmoid(z) * (1.0 / GLA_TEMP)
    r = lax.broadcasted_iota(jnp.int32, (tb, tb), 0)
    s = lax.broadcasted_iota(jnp.int32, (tb, tb), 1)
    same = (r // c) == (s // c)
    tri = jnp.where(same & ((s >= r) if reverse else (s <= r)), 1.0, 0.0).astype(F32)
    cum = jnp.dot(tri, log_a, preferred_element_type=F32, precision=lax.Precision.HIGHEST)
    rc = lax.broadcasted_iota(jnp.int32, (c, c), 0)
    sc = lax.broadcasted_iota(jnp.int32, (c, c), 1)
    keep = (sc >= rc) if reverse else (sc <= rc)

    order = range(nchunk - 1, -1, -1) if reverse else range(nchunk)
    for ci in order:
        rows = slice(ci * c, (ci + 1) * c)
        la = log_a[rows]
        cm = cum[rows]
        tot = jnp.sum(la, axis=0, keepdims=True)
        e_cum = jnp.exp(cm)
        e_inv = jnp.exp(-cm)
        e_end = jnp.exp(tot - cm)
        e_tot = jnp.exp(tot)
        for h in range(GLA_HEADS):
            hs = slice(h * GLA_DK, (h + 1) * GLA_DK)
            vs = slice(h * GLA_DV, (h + 1) * GLA_DV)
            q = q_ref[rows, hs].astype(F32) * (GLA_DK ** -0.5)
            k = k_ref[rows, hs].astype(F32)
            v = v_ref[rows, vs]
            qd = (q * e_cum[:, hs]).astype(BF16)
            ki = (k * e_inv[:, hs]).astype(BF16)
            ke = (k * e_end[:, hs]).astype(BF16)
            att = lax.dot_general(qd, ki, (((1,), (1,)), ((), ())), preferred_element_type=F32)
            att = jnp.where(keep, att, 0.0).astype(BF16)
            st = st_ref[h]
            o = jnp.dot(att, v, preferred_element_type=F32)
            o = o + lax.dot_general(qd, st.astype(BF16), (((1,), (1,)), ((), ())), preferred_element_type=F32)
            o_ref[rows, vs] = o
            upd = lax.dot_general(v, ke, (((0,), (0,)), ((), ())), preferred_element_type=F32)
            st_ref[h] = st * e_tot[:, hs] + upd


def _time_block_index(t, nblk, reverse):
    if not reverse:
        return t
    return jnp.where(t == 0, 0, nblk - t)


def _gla_call(p, wd_pad, bd, *, reverse, batch, t_len):
    m = p.shape[0]
    tb = TIME_BLOCK
    nblk = t_len // tb
    d = 1 if reverse else 0

    def rowblk(b, t):
        return b * nblk + _time_block_index(t, nblk, reverse)

    kern = functools.partial(_gla_kernel, reverse=reverse, tb=tb)
    return pl.pallas_call(
        kern,
        grid=(batch, nblk),
        in_specs=[pl.BlockSpec((tb, GLA_QK), lambda b, t: (rowblk(b, t), COL_GQ // GLA_QK)),
                  pl.BlockSpec((tb, GLA_QK), lambda b, t: (rowblk(b, t), COL_GK // GLA_QK)),
                  pl.BlockSpec((tb, GLA_V), lambda b, t: (rowblk(b, t), COL_GV // GLA_V)),
                  pl.BlockSpec((tb, LANES), lambda b, t: (rowblk(b, t), COL_LR // LANES)),
                  pl.BlockSpec((1, LANES, GLA_QK), lambda b, t: (d, 0, 0)),
                  pl.BlockSpec((1, 1, GLA_QK), lambda b, t: (d, 0, 0))],
        out_specs=pl.BlockSpec((tb, GLA_V), lambda b, t: (rowblk(b, t), 0)),
        out_shape=jax.ShapeDtypeStruct((m, GLA_V), F32),
        scratch_shapes=[pltpu.VMEM((GLA_HEADS, GLA_DV, GLA_DK), F32)],
        compiler_params=_cparams(("parallel", "arbitrary")),
        name="gla_bwd" if reverse else "gla_fwd",
    )(p, p, p, p, wd_pad, bd)


def _qkprep_kernel(q_ref, k_ref, cos_ref, sin_ref, gq_ref, gk_ref, qo_ref, ko_ref):
    cos = cos_ref[...]
    sin = sin_ref[...]
    lane = lax.broadcasted_iota(jnp.int32, (1, HEAD_DIM), 1)
    first = (lane % (HEAD_DIM // 2)) < (HEAD_DIM // 4)

    def prep(x, g, scale):
        y = _rms(x.astype(F32), g)
        partner = jnp.where(first, pltpu.roll(y, HEAD_DIM - HEAD_DIM // 4, 1), pltpu.roll(y, HEAD_DIM // 4, 1))
        out = y * cos + partner * sin
        if scale != 1.0:
            out = out * scale
        return out.astype(BF16)

    for h in range(ATT_Q_HEADS):
        hs = slice(h * HEAD_DIM, (h + 1) * HEAD_DIM)
        qo_ref[:, hs] = prep(q_ref[:, hs], gq_ref[...], HEAD_DIM ** -0.5)
    for h in range(ATT_KV_HEADS):
        hs = slice(h * HEAD_DIM, (h + 1) * HEAD_DIM)
        ko_ref[:, hs] = prep(k_ref[:, hs], gk_ref[...], 1.0)


def _qkprep_call(p, cos_t, sin_t, gq, gk, *, t_len):
    m = p.shape[0]
    tb = TIME_BLOCK
    nblk = t_len // tb
    return pl.pallas_call(
        _qkprep_kernel,
        grid=(m // tb,),
        in_specs=[pl.BlockSpec((tb, ATT_Q), lambda i: (i, COL_AQ // ATT_Q)),
                  pl.BlockSpec((tb, ATT_KV), lambda i: (i, COL_AK // ATT_KV)),
                  pl.BlockSpec((tb, HEAD_DIM), lambda i: (i % nblk, 0)),
                  pl.BlockSpec((tb, HEAD_DIM), lambda i: (i % nblk, 0)),
                  pl.BlockSpec((1, HEAD_DIM), lambda i: (0, 0)),
                  pl.BlockSpec((1, HEAD_DIM), lambda i: (0, 0))],
        out_specs=[pl.BlockSpec((tb, ATT_Q), lambda i: (i, 0)),
                   pl.BlockSpec((tb, ATT_KV), lambda i: (i, 0))],
        out_shape=[jax.ShapeDtypeStruct((m, ATT_Q), BF16), jax.ShapeDtypeStruct((m, ATT_KV), BF16)],
        compiler_params=_cparams(("parallel",)),
        name="qk_prep",
    )(p, p, cos_t, sin_t, gq, gk)


NEG_BIG = -1e30


def _attn_kernel(q_ref, k_ref, v_ref, o_ref, *, tq, tk, nctx, t_len):
    nkv = t_len // tk

    def run(masked):
        row = lax.broadcasted_iota(jnp.int32, (tq, 1), 0)
        for h in range(ATT_GROUP):
            hs = slice(h * HEAD_DIM, (h + 1) * HEAD_DIM)
            q = q_ref[:, hs]

            def body(j, carry):
                m_prev, l_prev, acc = carry
                start = pl.multiple_of(j * tk, tk)
                ks = k_ref[pl.ds(start, tk), :]
                vs = v_ref[pl.ds(start, tk), :]
                s = lax.dot_general(q, ks, (((1,), (1,)), ((), ())), preferred_element_type=F32)
                if masked:
                    thresh = jnp.where(j == 0, 0, nctx)
                    s = jnp.where(row >= thresh, s, NEG_BIG)
                m_new = jnp.maximum(m_prev, jnp.max(s, axis=-1, keepdims=True))
                alpha = jnp.exp(m_prev - m_new)
                pexp = jnp.exp(s - m_new)
                l_new = alpha * l_prev + jnp.sum(pexp, axis=-1, keepdims=True)
                acc = alpha * acc + jnp.dot(pexp.astype(BF16), vs, preferred_element_type=F32)
                return m_new, l_new, acc

            init = (jnp.full((tq, 1), NEG_BIG, F32), jnp.zeros((tq, 1), F32), jnp.zeros((tq, HEAD_DIM), F32))
            _, l_fin, acc = lax.fori_loop(0, nkv, body, init)
            o_ref[:, hs] = (acc / l_fin).astype(o_ref.dtype)

    first = pl.program_id(2) == 0
    pl.when(first)(lambda: run(True))
    pl.when(jnp.logical_not(first))(lambda: run(False))


def _attn_call(qn, kn, p, *, batch, t_len, nctx, tq):
    m = qn.shape[0]
    nq = t_len // tq
    tk = nctx
    width = ATT_GROUP * HEAD_DIM
    kern = functools.partial(_attn_kernel, tq=tq, tk=tk, nctx=nctx, t_len=t_len)
    return pl.pallas_call(
        kern,
        grid=(batch, ATT_KV_HEADS, nq),
        in_specs=[pl.BlockSpec((tq, width), lambda b, g, i: (b * nq + i, g)),
                  pl.BlockSpec((t_len, HEAD_DIM), lambda b, g, i: (b, g)),
                  pl.BlockSpec((t_len, HEAD_DIM), lambda b, g, i: (b, COL_AV // HEAD_DIM + g))],
        out_specs=pl.BlockSpec((tq, width), lambda b, g, i: (b * nq + i, g)),
        out_shape=jax.ShapeDtypeStruct((m, ATT_Q), BF16),
        compiler_params=_cparams(("parallel", "parallel", "arbitrary")),
        name="flash_attn",
    )(qn, kn, p)


def _halo_valid(blk, nblk):
    prev_ok = blk >= 2
    next_ok = jnp.logical_and(blk != 0, blk != nblk - 1)
    return prev_ok, next_ok


def _lru_kernel(x_ref, xp_ref, xn_ref, cw_ref, cb_ref, wg_ref, bg_ref, lam_ref, h_ref, carry_ref, *, reverse, tb, nblk):
    t = pl.program_id(1)
    blk = _time_block_index(t, nblk, reverse)

    @pl.when(t == 0)
    def _():
        carry_ref[...] = jnp.zeros_like(carry_ref)

    prev_ok, next_ok = _halo_valid(blk, nblk)
    x = x_ref[...].astype(F32)
    prev = jnp.where(prev_ok, xp_ref[...].astype(F32), 0.0)
    nxt = jnp.where(next_ok, xn_ref[...].astype(F32), 0.0)
    row = lax.broadcasted_iota(jnp.int32, (tb, 1), 0)
    xm1 = jnp.where(row == 0, prev[SUBLANES - 1:SUBLANES], pltpu.roll(x, 1, 0))
    xm2 = jnp.where(row == 0, prev[SUBLANES - 2:SUBLANES - 1],
                    jnp.where(row == 1, prev[SUBLANES - 1:SUBLANES], pltpu.roll(x, 2, 0)))
    xp1 = jnp.where(row == tb - 1, nxt[0:1], pltpu.roll(x, tb - 1, 0))
    xc = cw_ref[0:1] * xm2 + cw_ref[1:2] * xm1 + cw_ref[2:3] * x + cw_ref[3:4] * xp1 + cb_ref[...]

    xcb = xc.astype(BF16)
    zs = []
    for gi in range(2):
        parts = [jnp.dot(xcb[:, hb * LRU_BW:(hb + 1) * LRU_BW], wg_ref[0, gi, hb], preferred_element_type=F32)
                 for hb in range(LRU_BLOCKS)]
        zs.append(jnp.concatenate(parts, axis=-1) + bg_ref[0, gi:gi + 1, :])
    gate_r = _sigmoid(zs[0])
    gate_i = _sigmoid(zs[1])
    neg_lam = -lam_ref[0]
    softplus = jnp.maximum(neg_lam, 0.0) + jnp.log(1.0 + jnp.exp(-jnp.abs(neg_lam)))
    log_a = (-LRU_C) * gate_r * softplus
    a = jnp.exp(log_a)
    u = jnp.sqrt(jnp.tanh(-log_a) * (1.0 + a * a)) * (gate_i * xc)

    rowm = row % SUBLANES
    for sft in (1, 2, 4):
        if reverse:
            a_sh = pltpu.roll(a, tb - sft, 0)
            u_sh = pltpu.roll(u, tb - sft, 0)
            valid = rowm < SUBLANES - sft
        else:
            a_sh = pltpu.roll(a, sft, 0)
            u_sh = pltpu.roll(u, sft, 0)
            valid = rowm >= sft
        u = jnp.where(valid, u + a * u_sh, u)
        a = jnp.where(valid, a * a_sh, a)

    ntile = tb // SUBLANES
    carry = carry_ref[...]
    order = range(ntile - 1, -1, -1) if reverse else range(ntile)
    for j in order:
        rows = slice(j * SUBLANES, (j + 1) * SUBLANES)
        h_tile = u[rows] + a[rows] * carry
        h_ref[rows, :] = h_tile
        edge = h_tile[0:1] if reverse else h_tile[SUBLANES - 1:SUBLANES]
        carry = jnp.broadcast_to(edge, (SUBLANES, LRU_WIDTH))
    carry_ref[...] = carry


def _lru_call(p, cw, cb, wg, bg, lam, *, reverse, batch, t_len):
    m = p.shape[0]
    tb = TIME_BLOCK
    nblk = t_len // tb
    per = tb // SUBLANES
    nrow8 = m // SUBLANES
    d = 1 if reverse else 0

    def rowblk(b, t):
        return b * nblk + _time_block_index(t, nblk, reverse)

    kern = functools.partial(_lru_kernel, reverse=reverse, tb=tb, nblk=nblk)
    return pl.pallas_call(
        kern,
        grid=(batch, nblk),
        in_specs=[pl.BlockSpec((tb, LRU_WIDTH), lambda b, t: (rowblk(b, t), COL_LX // LRU_WIDTH)),
                  pl.BlockSpec((SUBLANES, LRU_WIDTH),
                               lambda b, t: (jnp.maximum(rowblk(b, t) * per - 1, 0), COL_LX // LRU_WIDTH)),
                  pl.BlockSpec((SUBLANES, LRU_WIDTH),
                               lambda b, t: (jnp.minimum((rowblk(b, t) + 1) * per, nrow8 - 1), COL_LX // LRU_WIDTH)),
                  pl.BlockSpec((LRU_CONV, LRU_WIDTH), lambda b, t: (0, 0)),
                  pl.BlockSpec((1, LRU_WIDTH), lambda b, t: (0, 0)),
                  pl.BlockSpec((1, 2, LRU_BLOCKS, LRU_BW, LRU_BW), lambda b, t: (d, 0, 0, 0, 0)),
                  pl.BlockSpec((1, 2, LRU_WIDTH), lambda b, t: (d, 0, 0)),
                  pl.BlockSpec((1, 1, LRU_WIDTH), lambda b, t: (d, 0, 0))],
        out_specs=pl.BlockSpec((tb, LRU_WIDTH), lambda b, t: (rowblk(b, t), 0)),
        out_shape=jax.ShapeDtypeStruct((m, LRU_WIDTH), F32),
        scratch_shapes=[pltpu.VMEM((SUBLANES, LRU_WIDTH), F32)],
        compiler_params=_cparams(("parallel", "arbitrary")),
        name="lru_bwd" if reverse else "lru_fwd",
    )(p, p, p, cw, cb, wg, bg, lam)


def _select_mod(mod_ref, idx, i, tm, tiles_per_batch, nctx):
    row = (i % tiles_per_batch) * tm + lax.broadcasted_iota(jnp.int32, (tm, 1), 0)
    return jnp.where(row < nctx, mod_ref[0, idx:idx + 1, :], mod_ref[0, 6 + idx:7 + idx, :])


def _merge_kernel(of_ref, ob_ref, r_ref, att_ref, hf_ref, hb_ref, y_ref, gate_ref, x_ref, mod_ref,
                  gn_ref, gp_ref, wb_ref, wo_ref, o_ref, *, tm, tiles_per_batch, nctx):
    i = pl.program_id(0)
    o = of_ref[...] + ob_ref[...]
    gla_parts = []
    for h in range(GLA_HEADS):
        vs = slice(h * GLA_DV, (h + 1) * GLA_DV)
        gla_parts.append(_rms(o[:, vs], gn_ref[:, vs]))
    gla = jnp.concatenate(gla_parts, axis=-1) * _silu(r_ref[...].astype(F32))
    lru = (hf_ref[...] + hb_ref[...]) * _gelu_tanh(y_ref[...].astype(F32))
    branches = (gla.astype(BF16), att_ref[...], lru.astype(BF16))
    mixed = None
    for bi in range(N_BRANCHES):
        gate = _sigmoid(gate_ref[:, bi * D_MODEL:(bi + 1) * D_MODEL].astype(F32))
        term = gate * jnp.dot(branches[bi], wb_ref[bi], preferred_element_type=F32)
        mixed = term if mixed is None else mixed + term
    y = jnp.dot(mixed.astype(BF16), wo_ref[...], preferred_element_type=F32)
    gate_vec = _select_mod(mod_ref, 2, i, tm, tiles_per_batch, nctx)
    o_ref[...] = x_ref[...] + gate_vec * _rms(y, gp_ref[...])


def _merge_call(o_f, o_b, p, att, h_f, h_b, x, modsel, gn, gp, wb, wo, *, tm, t_len, nctx):
    m, d = x.shape
    tiles_per_batch = t_len // tm
    kern = functools.partial(_merge_kernel, tm=tm, tiles_per_batch=tiles_per_batch, nctx=nctx)
    row = lambda i: (i, 0)
    const2 = lambda i: (0, 0)
    return pl.pallas_call(
        kern,
        grid=(m // tm,),
        in_specs=[pl.BlockSpec((tm, d), row),
                  pl.BlockSpec((tm, d), row),
                  pl.BlockSpec((tm, d), lambda i: (i, COL_GR // D_MODEL)),
                  pl.BlockSpec((tm, d), row),
                  pl.BlockSpec((tm, d), row),
                  pl.BlockSpec((tm, d), row),
                  pl.BlockSpec((tm, d), lambda i: (i, COL_LY // D_MODEL)),
                  pl.BlockSpec((tm, N_BRANCHES * d), lambda i: (i, COL_GATE // (N_BRANCHES * D_MODEL))),
                  pl.BlockSpec((tm, d), row),
                  pl.BlockSpec((1, 12, d), lambda i: (i // tiles_per_batch, 0, 0)),
                  pl.BlockSpec((1, d), const2),
                  pl.BlockSpec((1, d), const2),
                  pl.BlockSpec((N_BRANCHES, d, d), lambda i: (0, 0, 0)),
                  pl.BlockSpec((d, d), const2)],
        out_specs=pl.BlockSpec((tm, d), row),
        out_shape=jax.ShapeDtypeStruct((m, d), F32),
        compiler_params=_cparams(("parallel",)),
        name="merge_out",
    )(o_f, o_b, p, att, h_f, h_b, p, p, x, modsel, gn, gp, wb, wo)


def _ffn_down_kernel(u_ref, up_ref, un_ref, cw_ref, cb_ref, wd_ref, x_ref, mod_ref, gp_ref, o_ref,
                     *, tb, nblk):
    i = pl.program_id(0)
    blk = i % nblk
    prev_ok, next_ok = _halo_valid(blk, nblk)
    u = u_ref[...].astype(F32)
    prev = jnp.where(prev_ok, up_ref[SUBLANES - 1:SUBLANES, :].astype(F32), 0.0)
    nxt = jnp.where(next_ok, un_ref[0:1, :].astype(F32), 0.0)
    row = lax.broadcasted_iota(jnp.int32, (tb, 1), 0)
    um1 = jnp.where(row == 0, prev, pltpu.roll(u, 1, 0))
    up1 = jnp.where(row == tb - 1, nxt, pltpu.roll(u, tb - 1, 0))
    uc = cw_ref[0:1] * um1 + cw_ref[1:2] * u + cw_ref[2:3] * up1 + cb_ref[...]
    act = (_silu(uc[:, :D_FF]) * uc[:, D_FF:]).astype(BF16)
    y = jnp.dot(act, wd_ref[...], preferred_element_type=F32)
    gate_vec = _select_mod(mod_ref, 5, i, tb, nblk, tb)
    o_ref[...] = x_ref[...] + gate_vec * _rms(y, gp_ref[...])


def _ffn_down_call(u, cw, cb, wd, x, modsel, gp, *, t_len):
    m, d = x.shape
    tb = TIME_BLOCK
    nblk = t_len // tb
    per = tb // SUBLANES
    nrow8 = m // SUBLANES
    kern = functools.partial(_ffn_down_kernel, tb=tb, nblk=nblk)
    return pl.pallas_call(
        kern,
        grid=(m // tb,),
        in_specs=[pl.BlockSpec((tb, 2 * D_FF), lambda i: (i, 0)),
                  pl.BlockSpec((SUBLANES, 2 * D_FF), lambda i: (jnp.maximum(i * per - 1, 0), 0)),
                  pl.BlockSpec((SUBLANES, 2 * D_FF), lambda i: (jnp.minimum((i + 1) * per, nrow8 - 1), 0)),
                  pl.BlockSpec((FFN_CONV, 2 * D_FF), lambda i: (0, 0)),
                  pl.BlockSpec((1, 2 * D_FF), lambda i: (0, 0)),
                  pl.BlockSpec((D_FF, d), lambda i: (0, 0)),
                  pl.BlockSpec((tb, d), lambda i: (i, 0)),
                  pl.BlockSpec((1, 12, d), lambda i: (i // nblk, 0, 0)),
                  pl.BlockSpec((1, d), lambda i: (0, 0))],
        out_specs=pl.BlockSpec((tb, d), lambda i: (i, 0)),
        out_shape=jax.ShapeDtypeStruct((m, d), F32),
        compiler_params=_cparams(("parallel",)),
        name="ffn_down",
    )(u, u, u, cw, cb, wd, x, modsel, gp)


def _rope_tables(seq, nctx):
    rows = seq // GRID_W
    row = jnp.repeat(jnp.arange(rows), GRID_W)
    col = jnp.tile(jnp.arange(GRID_W), rows)
    axis_dim = HEAD_DIM // 2
    inv_freq = 1.0 / (ROPE_THETA ** (jnp.arange(0, axis_dim, 2, dtype=F32) / axis_dim))
    ang = jnp.stack([row, col], axis=-1).astype(F32)[:, :, None] * inv_freq
    cos, sin = jnp.cos(ang), jnp.sin(ang)
    cos_t = jnp.concatenate([cos[:, 0], cos[:, 0], cos[:, 1], cos[:, 1]], axis=-1)
    sin_t = jnp.concatenate([-sin[:, 0], sin[:, 0], -sin[:, 1], sin[:, 1]], axis=-1)
    cos_t = jnp.concatenate([jnp.ones((nctx, HEAD_DIM), F32), cos_t], axis=0)
    sin_t = jnp.concatenate([jnp.zeros((nctx, HEAD_DIM), F32), sin_t], axis=0)
    return cos_t, sin_t


def _permute_w_in(w):
    offs = [0]
    for s in IN_SPLITS:
        offs.append(offs[-1] + s)
    seg = lambda i: w[:, offs[i]:offs[i + 1]]
    gq, gk, gv, gr, lrf, lrb, aq, ak, av, lx, ly, gates = (seg(i) for i in range(12))
    pad = jnp.zeros((w.shape[0], D_IN_PAD - COL_LR - 2 * GLA_RANK), w.dtype)
    return jnp.concatenate([gq, gk, gv, gr, aq, lx, ly, gates, ak, av, lrf, lrb, pad], axis=1).astype(BF16)


def _largest_tile(t_len, cap):
    best = SUBLANES
    for cand in range(SUBLANES, cap + 1, SUBLANES):
        if t_len % cand == 0:
            best = cand
    return best


def kernel(x, c, ctx, c_ctx, w_ada, b_ada, norm_gains, w_in, gla_w_decay, gla_b_decay, gla_norm_g,
           att_q_norm_g, att_k_norm_g, lru_conv_w, lru_conv_b, lru_w_gates, lru_b_gates, lru_lambda,
           w_branch, w_out, ffn_w_up, ffn_conv_w, ffn_conv_b, ffn_w_down):
    batch, seq, d = x.shape
    nctx = ctx.shape[1]
    depth = w_ada.shape[0]
    assert d == D_MODEL and nctx == TIME_BLOCK and seq % TIME_BLOCK == 0 and seq % GRID_W == 0
    t_len = nctx + seq
    m = batch * t_len
    tm_big = _largest_tile(t_len, 1408)
    tq = _largest_tile(t_len, 768)
    assert tm_big % TIME_BLOCK == 0 or tm_big % LANES == 0
    assert tq % TIME_BLOCK == 0

    xc = jnp.concatenate([ctx, x], axis=1).reshape(m, d)

    rows = -(-(batch + 1) // SUBLANES) * SUBLANES
    cvec = jnp.zeros((rows, d), F32).at[:batch].set(c).at[batch].set(c_ctx)
    mod = _ada_call(cvec, w_ada, b_ada)
    mod = mod.reshape(depth, rows, 6, d)
    cos_t, sin_t = _rope_tables(seq, nctx)

    for l in range(depth):
        mod_c = jnp.broadcast_to(mod[l, batch][None], (batch, 6, d))
        modsel = jnp.concatenate([mod_c, mod[l, :batch]], axis=1)
        g_pre_mix, g_post_mix, g_pre_ffn, g_post_ffn = (norm_gains[l, i].reshape(1, d) for i in range(4))

        p = _nmm_call(xc, g_pre_mix, modsel, _permute_w_in(w_in[l]), shift_idx=0, scale_idx=1,
                      tm=tm_big, tn=1024, t_len=t_len, nctx=nctx, name="in_proj")

        wd_pad = jnp.zeros((2, LANES, GLA_QK), F32)
        wd_pad = wd_pad.at[0, :GLA_RANK].set(gla_w_decay[l, 0]).at[1, GLA_RANK:2 * GLA_RANK].set(gla_w_decay[l, 1])
        bd = gla_b_decay[l].reshape(2, 1, GLA_QK)
        o_f = _gla_call(p, wd_pad, bd, reverse=False, batch=batch, t_len=t_len)
        o_b = _gla_call(p, wd_pad, bd, reverse=True, batch=batch, t_len=t_len)

        qn, kn = _qkprep_call(p, cos_t, sin_t, att_q_norm_g[l].reshape(1, HEAD_DIM),
                              att_k_norm_g[l].reshape(1, HEAD_DIM), t_len=t_len)
        att = _attn_call(qn, kn, p, batch=batch, t_len=t_len, nctx=nctx, tq=tq)

        wg = lru_w_gates[l].astype(BF16)
        lam = lru_lambda[l].reshape(2, 1, LRU_WIDTH)
        cb = lru_conv_b[l].reshape(1, LRU_WIDTH)
        h_f = _lru_call(p, lru_conv_w[l], cb, wg, lru_b_gates[l], lam, reverse=False, batch=batch, t_len=t_len)
        h_b = _lru_call(p, lru_conv_w[l], cb, wg, lru_b_gates[l], lam, reverse=True, batch=batch, t_len=t_len)

        xc = _merge_call(o_f, o_b, p, att, h_f, h_b, xc, modsel, gla_norm_g[l].reshape(1, GLA_V), g_post_mix,
                         w_branch[l].astype(BF16), w_out[l].astype(BF16), tm=TIME_BLOCK, t_len=t_len, nctx=nctx)

        u = _nmm_call(xc, g_pre_ffn, modsel, ffn_w_up[l].astype(BF16), shift_idx=3, scale_idx=4,
                      tm=tm_big, tn=512, t_len=t_len, nctx=nctx, name="ffn_up")
        xc = _ffn_down_call(u, ffn_conv_w[l], ffn_conv_b[l].reshape(1, 2 * D_FF), ffn_w_down[l].astype(BF16),
                            xc, modsel, g_post_ffn, t_len=t_len)

    return xc.reshape(batch, t_len, d)[:, nctx:, :]
```

```python
import functools

import jax
import jax.numpy as jnp
from jax import lax
from jax.experimental import pallas as pl
from jax.experimental.pallas import tpu as pltpu

F32 = jnp.float32
BF16 = jnp.bfloat16

D_MODEL = 1024
NORM_EPS = 1e-6
N_BRANCHES = 3
GRID_W = 64

GLA_HEADS = 4
GLA_DK = D_MODEL // (2 * GLA_HEADS)
GLA_DV = D_MODEL // GLA_HEADS
GLA_QK = GLA_HEADS * GLA_DK
GLA_V = GLA_HEADS * GLA_DV
GLA_RANK = 16
GLA_TEMP = 16.0
GLA_CHUNK = 64

HEAD_DIM = 128
ATT_Q_HEADS = D_MODEL // HEAD_DIM
ATT_KV_HEADS = ATT_Q_HEADS // 4
ATT_GROUP = ATT_Q_HEADS // ATT_KV_HEADS
ATT_Q = ATT_Q_HEADS * HEAD_DIM
ATT_KV = ATT_KV_HEADS * HEAD_DIM
ROPE_THETA = 10000.0
LOG2_E = 1.4426950408889634

LRU_WIDTH = D_MODEL
LRU_BLOCKS = 8
LRU_BW = LRU_WIDTH // LRU_BLOCKS
LRU_C = 8.0
LRU_CONV = 4
LRU_CONV_LEFT = 2

D_FF = 2816
FFN_CONV = 3

IN_SPLITS = (GLA_QK, GLA_QK, GLA_V, GLA_V, GLA_RANK, GLA_RANK,
             ATT_Q, ATT_KV, ATT_KV, LRU_WIDTH, LRU_WIDTH, N_BRANCHES * D_MODEL)

SUBLANES = 8
LANES = 128
TIME_BLOCK = 256

COL_GQ = 0
COL_GK = 512
COL_GV = 1024
COL_GR = 2048
COL_AQ = 3072
COL_LX = 4096
COL_LY = 5120
COL_GATE = 6144
COL_AK = 9216
COL_AV = 9472
COL_LR = 9728
D_IN_PAD = 10240

VMEM_LIMIT = 56 * 1024 * 1024


def _cparams(sem):
    return pltpu.CompilerParams(dimension_semantics=sem, vmem_limit_bytes=VMEM_LIMIT)


def _sigmoid(x):
    return 1.0 / (1.0 + jnp.exp(-x))


def _silu(x):
    return x * _sigmoid(x)


def _gelu_tanh(x):
    return x * (0.5 * (1.0 + jnp.tanh(0.7978845608028654 * (x + 0.044715 * (x * x * x)))))


def _rms(x, g):
    ms = jnp.mean(x * x, axis=-1, keepdims=True)
    return x * lax.rsqrt(ms + NORM_EPS) * g


def _ada_kernel(c_ref, w_ref, b_ref, o_ref):
    c = c_ref[...]
    o_ref[0] = jnp.dot(_silu(c), w_ref[0], preferred_element_type=F32,
                       precision=lax.Precision.HIGHEST) + b_ref[0]


def _ada_call(cvec, w_ada, b_ada):
    depth, d, n = w_ada.shape
    tn = 1536
    rows = cvec.shape[0]
    return pl.pallas_call(
        _ada_kernel,
        grid=(depth, n // tn),
        in_specs=[pl.BlockSpec((rows, d), lambda l, j: (0, 0)),
                  pl.BlockSpec((1, d, tn), lambda l, j: (l, 0, j)),
                  pl.BlockSpec((1, 1, tn), lambda l, j: (l, 0, j))],
        out_specs=pl.BlockSpec((1, rows, tn), lambda l, j: (l, 0, j)),
        out_shape=jax.ShapeDtypeStruct((depth, rows, n), F32),
        compiler_params=_cparams(("parallel", "parallel")),
        name="ada_mod",
    )(cvec, w_ada, b_ada.reshape(depth, 1, n))


def _nmm_kernel(x_ref, g_ref, mod_ref, w_ref, o_ref, h_ref, *, shift_idx, scale_idx, tm, nctx, tiles_per_batch):
    i = pl.program_id(0)

    @pl.when(pl.program_id(1) == 0)
    def _():
        y = _rms(x_ref[...], g_ref[...])
        row = (i % tiles_per_batch) * tm + lax.broadcasted_iota(jnp.int32, (tm, 1), 0)
        is_ctx = row < nctx
        shift = jnp.where(is_ctx, mod_ref[0, shift_idx:shift_idx + 1, :], mod_ref[0, 6 + shift_idx:7 + shift_idx, :])
        scale = jnp.where(is_ctx, mod_ref[0, scale_idx:scale_idx + 1, :], mod_ref[0, 6 + scale_idx:7 + scale_idx, :])
        h_ref[...] = (y * (1.0 + scale) + shift).astype(BF16)

    o_ref[...] = jnp.dot(h_ref[...], w_ref[...], preferred_element_type=F32).astype(o_ref.dtype)


def _nmm_call(x, g, modsel, w, *, shift_idx, scale_idx, tm, tn, t_len, nctx, name):
    m, d = x.shape
    n = w.shape[1]
    tiles_per_batch = t_len // tm
    kern = functools.partial(_nmm_kernel, shift_idx=shift_idx, scale_idx=scale_idx, tm=tm, nctx=nctx,
                             tiles_per_batch=tiles_per_batch)
    return pl.pallas_call(
        kern,
        grid=(m // tm, n // tn),
        in_specs=[pl.BlockSpec((tm, d), lambda i, j: (i, 0)),
                  pl.BlockSpec((1, d), lambda i, j: (0, 0)),
                  pl.BlockSpec((1, 12, d), lambda i, j: (i // tiles_per_batch, 0, 0)),
                  pl.BlockSpec((d, tn), lambda i, j: (0, j))],
        out_specs=pl.BlockSpec((tm, tn), lambda i, j: (i, j)),
        out_shape=jax.ShapeDtypeStruct((m, n), BF16),
        scratch_shapes=[pltpu.VMEM((tm, d), BF16)],
        compiler_params=_cparams(("parallel", "arbitrary")),
        name=name,
    )(x, g, modsel, w)


def _log_sigmoid(z):
    return jnp.minimum(z, 0.0) - jnp.log(1.0 + jnp.exp(-jnp.abs(z)))


def _gla_kernel(q_ref, k_ref, v_ref, lr_ref, wd_ref, bd_ref, o_ref, st_ref, *, reverse, tb):
    c = GLA_CHUNK
    nchunk = tb // c

    @pl.when(pl.program_id(1) == 0)
    def _():
        st_ref[...] = jnp.zeros_like(st_ref)

    z = jnp.dot(lr_ref[...].astype(F32), wd_ref[0], preferred_element_type=F32,
                precision=lax.Precision.HIGHEST) + bd_ref[0]
    log_a = _log_sigmoid(z) * (1.0 / GLA_TEMP)
    r = lax.broadcasted_iota(jnp.int32, (tb, tb), 0)
    s = lax.broadcasted_iota(jnp.int32, (tb, tb), 1)
    same = (r // c) == (s // c)
    tri = jnp.where(same & ((s >= r) if reverse else (s <= r)), 1.0, 0.0).astype(F32)
    cum = jnp.dot(tri, log_a, preferred_element_type=F32, precision=lax.Precision.HIGHEST)
    rc = lax.broadcasted_iota(jnp.int32, (c, c), 0)
    sc = lax.broadcasted_iota(jnp.int32, (c, c), 1)
    keep = (sc >= rc) if reverse else (sc <= rc)

    order = range(nchunk - 1, -1, -1) if reverse else range(nchunk)
    for ci in order:
        rows = slice(ci * c, (ci + 1) * c)
        la = log_a[rows]
        cm = cum[rows]
        tot = jnp.sum(la, axis=0, keepdims=True)
        e_cum = jnp.exp(cm)
        e_inv = jnp.exp(-cm)
        e_end = jnp.exp(tot - cm)
        e_tot = jnp.exp(tot)
        for h in range(GLA_HEADS):
            hs = slice(h * GLA_DK, (h + 1) * GLA_DK)
            vs = slice(h * GLA_DV, (h + 1) * GLA_DV)
            q = q_ref[rows, hs].astype(F32) * (GLA_DK ** -0.5)
            k = k_ref[rows, hs].astype(F32)
            v = v_ref[rows, vs]
            qd = (q * e_cum[:, hs]).astype(BF16)
            ki = (k * e_inv[:, hs]).astype(BF16)
            ke = (k * e_end[:, hs]).astype(BF16)
            att = lax.dot_general(qd, ki, (((1,), (1,)), ((), ())), preferred_element_type=F32)
            att = jnp.where(keep, att, 0.0).astype(BF16)
            st = st_ref[h]
            o = jnp.dot(att, v, preferred_element_type=F32)
            o = o + lax.dot_general(qd, st.astype(BF16), (((1,), (1,)), ((), ())), preferred_element_type=F32)
            o_ref[rows, vs] = o
            upd = lax.dot_general(v, ke, (((0,), (0,)), ((), ())), preferred_element_type=F32)
            st_ref[h] = st * e_tot[:, hs] + upd


def _time_block_index(t, nblk, reverse):
    if not reverse:
        return t
    return jnp.where(t == 0, 0, nblk - t)


def _gla_call(p, wd_pad, bd, *, reverse, batch, t_len):
    m = p.shape[0]
    tb = TIME_BLOCK
    nblk = t_len // tb
    d = 1 if reverse else 0

    def rowblk(b, t):
        return b * nblk + _time_block_index(t, nblk, reverse)

    kern = functools.partial(_gla_kernel, reverse=reverse, tb=tb)
    return pl.pallas_call(
        kern,
        grid=(batch, nblk),
        in_specs=[pl.BlockSpec((tb, GLA_QK), lambda b, t: (rowblk(b, t), COL_GQ // GLA_QK)),
                  pl.BlockSpec((tb, GLA_QK), lambda b, t: (rowblk(b, t), COL_GK // GLA_QK)),
                  pl.BlockSpec((tb, GLA_V), lambda b, t: (rowblk(b, t), COL_GV // GLA_V)),
                  pl.BlockSpec((tb, LANES), lambda b, t: (rowblk(b, t), COL_LR // LANES)),
                  pl.BlockSpec((1, LANES, GLA_QK), lambda b, t: (d, 0, 0)),
                  pl.BlockSpec((1, 1, GLA_QK), lambda b, t: (d, 0, 0))],
        out_specs=pl.BlockSpec((tb, GLA_V), lambda b, t: (rowblk(b, t), 0)),
        out_shape=jax.ShapeDtypeStruct((m, GLA_V), F32),
        scratch_shapes=[pltpu.VMEM((GLA_HEADS, GLA_DV, GLA_DK), F32)],
        compiler_params=_cparams(("parallel", "arbitrary")),
        name="gla_bwd" if reverse else "gla_fwd",
    )(p, p, p, p, wd_pad, bd)


def _qkprep_kernel(q_ref, k_ref, v_ref, cos_ref, sin_ref, gq_ref, gk_ref, qo_ref, ko_ref, vo_ref):
    cos = cos_ref[...]
    sin = sin_ref[...]
    lane = lax.broadcasted_iota(jnp.int32, (1, HEAD_DIM), 1)
    first = (lane % (HEAD_DIM // 2)) < (HEAD_DIM // 4)

    def prep(x, g, scale):
        y = _rms(x.astype(F32), g)
        partner = jnp.where(first, pltpu.roll(y, HEAD_DIM - HEAD_DIM // 4, 1), pltpu.roll(y, HEAD_DIM // 4, 1))
        out = y * cos + partner * sin
        if scale != 1.0:
            out = out * scale
        return out.astype(BF16)

    for h in range(ATT_Q_HEADS):
        hs = slice(h * HEAD_DIM, (h + 1) * HEAD_DIM)
        qo_ref[:, hs] = prep(q_ref[:, hs], gq_ref[...], HEAD_DIM ** -0.5 * LOG2_E)
    for h in range(ATT_KV_HEADS):
        hs = slice(h * HEAD_DIM, (h + 1) * HEAD_DIM)
        ko_ref[:, hs] = prep(k_ref[:, hs], gk_ref[...], 1.0)
        vo_ref[:, 2 * h * HEAD_DIM:(2 * h + 1) * HEAD_DIM] = v_ref[:, hs]
        vo_ref[:, (2 * h + 1) * HEAD_DIM:(2 * h + 2) * HEAD_DIM] = jnp.ones((v_ref.shape[0], HEAD_DIM), BF16)


def _qkprep_call(p, cos_t, sin_t, gq, gk, *, t_len):
    m = p.shape[0]
    tb = TIME_BLOCK
    nblk = t_len // tb
    return pl.pallas_call(
        _qkprep_kernel,
        grid=(m // tb,),
        in_specs=[pl.BlockSpec((tb, ATT_Q), lambda i: (i, COL_AQ // ATT_Q)),
                  pl.BlockSpec((tb, ATT_KV), lambda i: (i, COL_AK // ATT_KV)),
                  pl.BlockSpec((tb, ATT_KV), lambda i: (i, COL_AV // ATT_KV)),
                  pl.BlockSpec((tb, HEAD_DIM), lambda i: (i % nblk, 0)),
                  pl.BlockSpec((tb, HEAD_DIM), lambda i: (i % nblk, 0)),
                  pl.BlockSpec((1, HEAD_DIM), lambda i: (0, 0)),
                  pl.BlockSpec((1, HEAD_DIM), lambda i: (0, 0))],
        out_specs=[pl.BlockSpec((tb, ATT_Q), lambda i: (i, 0)),
                   pl.BlockSpec((tb, ATT_KV), lambda i: (i, 0)),
                   pl.BlockSpec((tb, 2 * ATT_KV), lambda i: (i, 0))],
        out_shape=[jax.ShapeDtypeStruct((m, ATT_Q), BF16), jax.ShapeDtypeStruct((m, ATT_KV), BF16),
                   jax.ShapeDtypeStruct((m, 2 * ATT_KV), BF16)],
        compiler_params=_cparams(("parallel",)),
        name="qk_prep",
    )(p, p, p, cos_t, sin_t, gq, gk)


NEG_BIG = -1e30


ATT_ROW_SUB = 64
NT_DIMS = (((1,), (1,)), ((), ()))


def _attn_kernel(q_ref, k_ref, v_ref, o_ref, s_ref, p_ref, a_ref, m_ref, acc_ref, *, tq, tk, t_len):
    nkv = t_len // tk
    sub = ATT_ROW_SUB
    nlt = tk // LANES
    q = q_ref[...]
    m_ref[...] = jnp.full_like(m_ref, NEG_BIG)
    acc_ref[...] = jnp.zeros_like(acc_ref)

    def scores(j):
        s_ref[j % 2] = lax.dot_general(q, k_ref[j * tk:(j + 1) * tk, :], NT_DIMS, preferred_element_type=F32)

    def exponentials(j):
        slot = j % 2
        for r in range(tq // sub):
            rows = slice(r * sub, (r + 1) * sub)
            tiles = [s_ref[slot, rows, t * LANES:(t + 1) * LANES] for t in range(nlt)]
            mx = tiles[0]
            for t in range(1, nlt):
                mx = jnp.maximum(mx, tiles[t])
            m_prev = m_ref[rows, :]
            m_new = jnp.maximum(m_prev, jnp.max(mx, axis=1, keepdims=True))
            a_ref[slot, rows, :] = jnp.exp2(m_prev - m_new)
            m_ref[rows, :] = m_new
            for t in range(nlt):
                p_ref[slot, rows, t * LANES:(t + 1) * LANES] = jnp.exp2(tiles[t] - m_new).astype(BF16)

    def weighted_values(j):
        slot = j % 2
        pv = jnp.dot(p_ref[slot], v_ref[j * tk:(j + 1) * tk, :], preferred_element_type=F32)
        alpha = a_ref[slot]
        acc_ref[...] = jnp.concatenate([alpha, alpha], axis=1) * acc_ref[...] + pv

    for step in range(nkv + 2):
        if step < nkv:
            scores(step)
        if 1 <= step <= nkv:
            exponentials(step - 1)
        if step >= 2:
            weighted_values(step - 2)

    acc = acc_ref[...]
    o_ref[...] = (acc[:, :HEAD_DIM] / acc[:, HEAD_DIM:]).astype(o_ref.dtype)


def _attn_call(qn, kn, vext, *, batch, t_len, tq):
    m = qn.shape[0]
    nq = t_len // tq
    tk = tq
    assert tq % ATT_ROW_SUB == 0 and tk % LANES == 0
    kern = functools.partial(_attn_kernel, tq=tq, tk=tk, t_len=t_len)
    return pl.pallas_call(
        kern,
        grid=(batch, ATT_KV_HEADS, nq, ATT_GROUP),
        in_specs=[pl.BlockSpec((tq, HEAD_DIM), lambda b, g, i, h: (b * nq + i, g * ATT_GROUP + h)),
                  pl.BlockSpec((t_len, HEAD_DIM), lambda b, g, i, h: (b, g)),
                  pl.BlockSpec((t_len, 2 * HEAD_DIM), lambda b, g, i, h: (b, g))],
        out_specs=pl.BlockSpec((tq, HEAD_DIM), lambda b, g, i, h: (b * nq + i, g * ATT_GROUP + h)),
        out_shape=jax.ShapeDtypeStruct((m, ATT_Q), BF16),
        scratch_shapes=[pltpu.VMEM((2, tq, tk), F32), pltpu.VMEM((2, tq, tk), BF16),
                        pltpu.VMEM((2, tq, LANES), F32), pltpu.VMEM((tq, LANES), F32),
                        pltpu.VMEM((tq, 2 * HEAD_DIM), F32)],
        compiler_params=_cparams(("parallel", "parallel", "arbitrary", "arbitrary")),
        name="flash_attn",
    )(qn, kn, vext)


def _attn_ctx_kernel(q_ref, k_ref, v_ref, att_ref, o_ref):
    del att_ref
    k = k_ref[...]
    v = v_ref[...]
    for h in range(ATT_GROUP):
        hs = slice(h * HEAD_DIM, (h + 1) * HEAD_DIM)
        s = lax.dot_general(q_ref[:, hs], k, NT_DIMS, preferred_element_type=F32)
        pexp = jnp.exp2(s - jnp.max(s, axis=1, keepdims=True)).astype(BF16)
        acc = jnp.dot(pexp, v, preferred_element_type=F32)
        o_ref[:, hs] = (acc[:, :HEAD_DIM] / acc[:, HEAD_DIM:]).astype(o_ref.dtype)


def _attn_ctx_call(qn, kn, vext, att, *, batch, t_len, nctx):
    nblk = t_len // nctx
    width = ATT_GROUP * HEAD_DIM
    return pl.pallas_call(
        _attn_ctx_kernel,
        grid=(batch, ATT_KV_HEADS),
        in_specs=[pl.BlockSpec((nctx, width), lambda b, g: (b * nblk, g)),
                  pl.BlockSpec((nctx, HEAD_DIM), lambda b, g: (b * nblk, g)),
                  pl.BlockSpec((nctx, 2 * HEAD_DIM), lambda b, g: (b * nblk, g)),
                  pl.BlockSpec(memory_space=pl.ANY)],
        out_specs=pl.BlockSpec((nctx, width), lambda b, g: (b * nblk, g)),
        out_shape=jax.ShapeDtypeStruct(att.shape, att.dtype),
        input_output_aliases={3: 0},
        compiler_params=_cparams(("parallel", "parallel")),
        name="ctx_attn",
    )(qn, kn, vext, att)


def _halo_valid(blk, nblk):
    prev_ok = blk >= 2
    next_ok = jnp.logical_and(blk != 0, blk != nblk - 1)
    return prev_ok, next_ok


def _lru_kernel(x_ref, xp_ref, xn_ref, cw_ref, cb_ref, wg_ref, bg_ref, lam_ref, h_ref, carry_ref, *, reverse, tb, nblk):
    t = pl.program_id(1)
    blk = _time_block_index(t, nblk, reverse)

    @pl.when(t == 0)
    def _():
        carry_ref[...] = jnp.zeros_like(carry_ref)

    prev_ok, next_ok = _halo_valid(blk, nblk)
    x = x_ref[...].astype(F32)
    prev = jnp.where(prev_ok, xp_ref[...].astype(F32), 0.0)
    nxt = jnp.where(next_ok, xn_ref[...].astype(F32), 0.0)
    row = lax.broadcasted_iota(jnp.int32, (tb, 1), 0)
    xm1 = jnp.where(row == 0, prev[SUBLANES - 1:SUBLANES], pltpu.roll(x, 1, 0))
    xm2 = jnp.where(row == 0, prev[SUBLANES - 2:SUBLANES - 1],
                    jnp.where(row == 1, prev[SUBLANES - 1:SUBLANES], pltpu.roll(x, 2, 0)))
    xp1 = jnp.where(row == tb - 1, nxt[0:1], pltpu.roll(x, tb - 1, 0))
    xc = cw_ref[0:1] * xm2 + cw_ref[1:2] * xm1 + cw_ref[2:3] * x + cw_ref[3:4] * xp1 + cb_ref[...]

    xcb = xc.astype(BF16)
    zs = []
    for gi in range(2):
        parts = [jnp.dot(xcb[:, hb * LRU_BW:(hb + 1) * LRU_BW], wg_ref[0, gi, hb], preferred_element_type=F32)
                 for hb in range(LRU_BLOCKS)]
        zs.append(jnp.concatenate(parts, axis=-1) + bg_ref[0, gi:gi + 1, :])
    gate_r = _sigmoid(zs[0])
    gate_i = _sigmoid(zs[1])
    neg_lam = -lam_ref[0]
    softplus = jnp.maximum(neg_lam, 0.0) + jnp.log(1.0 + jnp.exp(-jnp.abs(neg_lam)))
    log_a = (-LRU_C) * gate_r * softplus
    a = jnp.exp(log_a)
    u = jnp.sqrt(jnp.tanh(-log_a) * (1.0 + a * a)) * (gate_i * xc)

    rowm = row % SUBLANES
    for sft in (1, 2, 4):
        if reverse:
            a_sh = pltpu.roll(a, tb - sft, 0)
            u_sh = pltpu.roll(u, tb - sft, 0)
            valid = rowm < SUBLANES - sft
        else:
            a_sh = pltpu.roll(a, sft, 0)
            u_sh = pltpu.roll(u, sft, 0)
            valid = rowm >= sft
        u = jnp.where(valid, u + a * u_sh, u)
        a = jnp.where(valid, a * a_sh, a)

    ntile = tb // SUBLANES
    carry = carry_ref[...]
    order = range(ntile - 1, -1, -1) if reverse else range(ntile)
    for j in order:
        rows = slice(j * SUBLANES, (j + 1) * SUBLANES)
        h_tile = u[rows] + a[rows] * carry
        h_ref[rows, :] = h_tile
        edge = h_tile[0:1] if reverse else h_tile[SUBLANES - 1:SUBLANES]
        carry = jnp.broadcast_to(edge, (SUBLANES, LRU_WIDTH))
    carry_ref[...] = carry


def _lru_call(p, cw, cb, wg, bg, lam, *, reverse, batch, t_len):
    m = p.shape[0]
    tb = TIME_BLOCK
    nblk = t_len // tb
    per = tb // SUBLANES
    nrow8 = m // SUBLANES
    d = 1 if reverse else 0

    def rowblk(b, t):
        return b * nblk + _time_block_index(t, nblk, reverse)

    kern = functools.partial(_lru_kernel, reverse=reverse, tb=tb, nblk=nblk)
    return pl.pallas_call(
        kern,
        grid=(batch, nblk),
        in_specs=[pl.BlockSpec((tb, LRU_WIDTH), lambda b, t: (rowblk(b, t), COL_LX // LRU_WIDTH)),
                  pl.BlockSpec((SUBLANES, LRU_WIDTH),
                               lambda b, t: (jnp.maximum(rowblk(b, t) * per - 1, 0), COL_LX // LRU_WIDTH)),
                  pl.BlockSpec((SUBLANES, LRU_WIDTH),
                               lambda b, t: (jnp.minimum((rowblk(b, t) + 1) * per, nrow8 - 1), COL_LX // LRU_WIDTH)),
                  pl.BlockSpec((LRU_CONV, LRU_WIDTH), lambda b, t: (0, 0)),
                  pl.BlockSpec((1, LRU_WIDTH), lambda b, t: (0, 0)),
                  pl.BlockSpec((1, 2, LRU_BLOCKS, LRU_BW, LRU_BW), lambda b, t: (d, 0, 0, 0, 0)),
                  pl.BlockSpec((1, 2, LRU_WIDTH), lambda b, t: (d, 0, 0)),
                  pl.BlockSpec((1, 1, LRU_WIDTH), lambda b, t: (d, 0, 0))],
        out_specs=pl.BlockSpec((tb, LRU_WIDTH), lambda b, t: (rowblk(b, t), 0)),
        out_shape=jax.ShapeDtypeStruct((m, LRU_WIDTH), F32),
        scratch_shapes=[pltpu.VMEM((SUBLANES, LRU_WIDTH), F32)],
        compiler_params=_cparams(("parallel", "arbitrary")),
        name="lru_bwd" if reverse else "lru_fwd",
    )(p, p, p, cw, cb, wg, bg, lam)


def _select_mod(mod_ref, idx, i, tm, tiles_per_batch, nctx):
    row = (i % tiles_per_batch) * tm + lax.broadcasted_iota(jnp.int32, (tm, 1), 0)
    return jnp.where(row < nctx, mod_ref[0, idx:idx + 1, :], mod_ref[0, 6 + idx:7 + idx, :])


def _merge_kernel(of_ref, ob_ref, r_ref, att_ref, hf_ref, hb_ref, y_ref, gate_ref, x_ref, mod_ref,
                  gn_ref, gp_ref, wb_ref, wo_ref, o_ref, *, tm, tiles_per_batch, nctx):
    i = pl.program_id(0)
    o = of_ref[...] + ob_ref[...]
    gla_parts = []
    for h in range(GLA_HEADS):
        vs = slice(h * GLA_DV, (h + 1) * GLA_DV)
        gla_parts.append(_rms(o[:, vs], gn_ref[:, vs]))
    gla = jnp.concatenate(gla_parts, axis=-1) * _silu(r_ref[...].astype(F32))
    lru = (hf_ref[...] + hb_ref[...]) * _gelu_tanh(y_ref[...].astype(F32))
    branches = (gla.astype(BF16), att_ref[...], lru.astype(BF16))
    mixed = None
    for bi in range(N_BRANCHES):
        gate = _sigmoid(gate_ref[:, bi * D_MODEL:(bi + 1) * D_MODEL].astype(F32))
        term = gate * jnp.dot(branches[bi], wb_ref[bi], preferred_element_type=F32)
        mixed = term if mixed is None else mixed + term
    y = jnp.dot(mixed.astype(BF16), wo_ref[...], preferred_element_type=F32)
    gate_vec = _select_mod(mod_ref, 2, i, tm, tiles_per_batch, nctx)
    o_ref[...] = x_ref[...] + gate_vec * _rms(y, gp_ref[...])


def _merge_call(o_f, o_b, p, att, h_f, h_b, x, modsel, gn, gp, wb, wo, *, tm, t_len, nctx):
    m, d = x.shape
    tiles_per_batch = t_len // tm
    kern = functools.partial(_merge_kernel, tm=tm, tiles_per_batch=tiles_per_batch, nctx=nctx)
    row = lambda i: (i, 0)
    const2 = lambda i: (0, 0)
    return pl.pallas_call(
        kern,
        grid=(m // tm,),
        in_specs=[pl.BlockSpec((tm, d), row),
                  pl.BlockSpec((tm, d), row),
                  pl.BlockSpec((tm, d), lambda i: (i, COL_GR // D_MODEL)),
                  pl.BlockSpec((tm, d), row),
                  pl.BlockSpec((tm, d), row),
                  pl.BlockSpec((tm, d), row),
                  pl.BlockSpec((tm, d), lambda i: (i, COL_LY // D_MODEL)),
                  pl.BlockSpec((tm, N_BRANCHES * d), lambda i: (i, COL_GATE // (N_BRANCHES * D_MODEL))),
                  pl.BlockSpec((tm, d), row),
                  pl.BlockSpec((1, 12, d), lambda i: (i // tiles_per_batch, 0, 0)),
                  pl.BlockSpec((1, d), const2),
                  pl.BlockSpec((1, d), const2),
                  pl.BlockSpec((N_BRANCHES, d, d), lambda i: (0, 0, 0)),
                  pl.BlockSpec((d, d), const2)],
        out_specs=pl.BlockSpec((tm, d), row),
        out_shape=jax.ShapeDtypeStruct((m, d), F32),
        compiler_params=_cparams(("parallel",)),
        name="merge_out",
    )(o_f, o_b, p, att, h_f, h_b, p, p, x, modsel, gn, gp, wb, wo)


def _ffn_down_kernel(u_ref, up_ref, un_ref, cw_ref, cb_ref, wd_ref, x_ref, mod_ref, gp_ref, o_ref,
                     *, tb, nblk):
    i = pl.program_id(0)
    blk = i % nblk
    prev_ok, next_ok = _halo_valid(blk, nblk)
    u = u_ref[...].astype(F32)
    prev = jnp.where(prev_ok, up_ref[SUBLANES - 1:SUBLANES, :].astype(F32), 0.0)
    nxt = jnp.where(next_ok, un_ref[0:1, :].astype(F32), 0.0)
    row = lax.broadcasted_iota(jnp.int32, (tb, 1), 0)
    um1 = jnp.where(row == 0, prev, pltpu.roll(u, 1, 0))
    up1 = jnp.where(row == tb - 1, nxt, pltpu.roll(u, tb - 1, 0))
    uc = cw_ref[0:1] * um1 + cw_ref[1:2] * u + cw_ref[2:3] * up1 + cb_ref[...]
    act = (_silu(uc[:, :D_FF]) * uc[:, D_FF:]).astype(BF16)
    y = jnp.dot(act, wd_ref[...], preferred_element_type=F32)
    gate_vec = _select_mod(mod_ref, 5, i, tb, nblk, tb)
    o_ref[...] = x_ref[...] + gate_vec * _rms(y, gp_ref[...])


def _ffn_down_call(u, cw, cb, wd, x, modsel, gp, *, t_len):
    m, d = x.shape
    tb = TIME_BLOCK
    nblk = t_len // tb
    per = tb // SUBLANES
    nrow8 = m // SUBLANES
    kern = functools.partial(_ffn_down_kernel, tb=tb, nblk=nblk)
    return pl.pallas_call(
        kern,
        grid=(m // tb,),
        in_specs=[pl.BlockSpec((tb, 2 * D_FF), lambda i: (i, 0)),
                  pl.BlockSpec((SUBLANES, 2 * D_FF), lambda i: (jnp.maximum(i * per - 1, 0), 0)),
                  pl.BlockSpec((SUBLANES, 2 * D_FF), lambda i: (jnp.minimum((i + 1) * per, nrow8 - 1), 0)),
                  pl.BlockSpec((FFN_CONV, 2 * D_FF), lambda i: (0, 0)),
                  pl.BlockSpec((1, 2 * D_FF), lambda i: (0, 0)),
                  pl.BlockSpec((D_FF, d), lambda i: (0, 0)),
                  pl.BlockSpec((tb, d), lambda i: (i, 0)),
                  pl.BlockSpec((1, 12, d), lambda i: (i // nblk, 0, 0)),
                  pl.BlockSpec((1, d), lambda i: (0, 0))],
        out_specs=pl.BlockSpec((tb, d), lambda i: (i, 0)),
        out_shape=jax.ShapeDtypeStruct((m, d), F32),
        compiler_params=_cparams(("parallel",)),
        name="ffn_down",
    )(u, u, u, cw, cb, wd, x, modsel, gp)


def _rope_tables(seq, nctx):
    rows = seq // GRID_W
    row = jnp.repeat(jnp.arange(rows), GRID_W)
    col = jnp.tile(jnp.arange(GRID_W), rows)
    axis_dim = HEAD_DIM // 2
    inv_freq = 1.0 / (ROPE_THETA ** (jnp.arange(0, axis_dim, 2, dtype=F32) / axis_dim))
    ang = jnp.stack([row, col], axis=-1).astype(F32)[:, :, None] * inv_freq
    cos, sin = jnp.cos(ang), jnp.sin(ang)
    cos_t = jnp.concatenate([cos[:, 0], cos[:, 0], cos[:, 1], cos[:, 1]], axis=-1)
    sin_t = jnp.concatenate([-sin[:, 0], sin[:, 0], -sin[:, 1], sin[:, 1]], axis=-1)
    cos_t = jnp.concatenate([jnp.ones((nctx, HEAD_DIM), F32), cos_t], axis=0)
    sin_t = jnp.concatenate([jnp.zeros((nctx, HEAD_DIM), F32), sin_t], axis=0)
    return cos_t, sin_t


def _permute_w_in(w):
    offs = [0]
    for s in IN_SPLITS:
        offs.append(offs[-1] + s)
    seg = lambda i: w[:, offs[i]:offs[i + 1]]
    gq, gk, gv, gr, lrf, lrb, aq, ak, av, lx, ly, gates = (seg(i) for i in range(12))
    pad = jnp.zeros((w.shape[0], D_IN_PAD - COL_LR - 2 * GLA_RANK), w.dtype)
    return jnp.concatenate([gq, gk, gv, gr, aq, lx, ly, gates, ak, av, lrf, lrb, pad], axis=1).astype(BF16)


def _largest_tile(t_len, cap):
    best = SUBLANES
    for cand in range(SUBLANES, cap + 1, SUBLANES):
        if t_len % cand == 0:
            best = cand
    return best


def kernel(x, c, ctx, c_ctx, w_ada, b_ada, norm_gains, w_in, gla_w_decay, gla_b_decay, gla_norm_g,
           att_q_norm_g, att_k_norm_g, lru_conv_w, lru_conv_b, lru_w_gates, lru_b_gates, lru_lambda,
           w_branch, w_out, ffn_w_up, ffn_conv_w, ffn_conv_b, ffn_w_down):
    batch, seq, d = x.shape
    nctx = ctx.shape[1]
    depth = w_ada.shape[0]
    assert d == D_MODEL and nctx == TIME_BLOCK and seq % TIME_BLOCK == 0 and seq % GRID_W == 0
    t_len = nctx + seq
    m = batch * t_len
    tm_big = _largest_tile(t_len, 1408)
    tq = _largest_tile(t_len, 768)
    assert tm_big % TIME_BLOCK == 0 or tm_big % LANES == 0
    assert tq % TIME_BLOCK == 0

    xc = jnp.concatenate([ctx, x], axis=1).reshape(m, d)

    rows = -(-(batch + 1) // SUBLANES) * SUBLANES
    cvec = jnp.zeros((rows, d), F32).at[:batch].set(c).at[batch].set(c_ctx)
    mod = _ada_call(cvec, w_ada, b_ada)
    mod = mod.reshape(depth, rows, 6, d)
    cos_t, sin_t = _rope_tables(seq, nctx)

    for l in range(depth):
        mod_c = jnp.broadcast_to(mod[l, batch][None], (batch, 6, d))
        modsel = jnp.concatenate([mod_c, mod[l, :batch]], axis=1)
        g_pre_mix, g_post_mix, g_pre_ffn, g_post_ffn = (norm_gains[l, i].reshape(1, d) for i in range(4))

        p = _nmm_call(xc, g_pre_mix, modsel, _permute_w_in(w_in[l]), shift_idx=0, scale_idx=1,
                      tm=tm_big, tn=1024, t_len=t_len, nctx=nctx, name="in_proj")

        wd_pad = jnp.zeros((2, LANES, GLA_QK), F32)
        wd_pad = wd_pad.at[0, :GLA_RANK].set(gla_w_decay[l, 0]).at[1, GLA_RANK:2 * GLA_RANK].set(gla_w_decay[l, 1])
        bd = gla_b_decay[l].reshape(2, 1, GLA_QK)
        o_f = _gla_call(p, wd_pad, bd, reverse=False, batch=batch, t_len=t_len)
        o_b = _gla_call(p, wd_pad, bd, reverse=True, batch=batch, t_len=t_len)

        qn, kn, vext = _qkprep_call(p, cos_t, sin_t, att_q_norm_g[l].reshape(1, HEAD_DIM),
                                    att_k_norm_g[l].reshape(1, HEAD_DIM), t_len=t_len)
        att = _attn_call(qn, kn, vext, batch=batch, t_len=t_len, tq=tq)
        if l < depth - 1:
            att = _attn_ctx_call(qn, kn, vext, att, batch=batch, t_len=t_len, nctx=nctx)

        wg = lru_w_gates[l].astype(BF16)
        lam = lru_lambda[l].reshape(2, 1, LRU_WIDTH)
        cb = lru_conv_b[l].reshape(1, LRU_WIDTH)
        h_f = _lru_call(p, lru_conv_w[l], cb, wg, lru_b_gates[l], lam, reverse=False, batch=batch, t_len=t_len)
        h_b = _lru_call(p, lru_conv_w[l], cb, wg, lru_b_gates[l], lam, reverse=True, batch=batch, t_len=t_len)

        xc = _merge_call(o_f, o_b, p, att, h_f, h_b, xc, modsel, gla_norm_g[l].reshape(1, GLA_V), g_post_mix,
                         w_branch[l].astype(BF16), w_out[l].astype(BF16), tm=TIME_BLOCK, t_len=t_len, nctx=nctx)

        u = _nmm_call(xc, g_pre_ffn, modsel, ffn_w_up[l].astype(BF16), shift_idx=3, scale_idx=4,
                      tm=tm_big, tn=512, t_len=t_len, nctx=nctx, name="ffn_up")
        xc = _ffn_down_call(u, ffn_conv_w[l], ffn_conv_b[l].reshape(1, 2 * D_FF), ffn_w_down[l].astype(BF16),
                            xc, modsel, g_post_ffn, t_len=t_len)

    return xc.reshape(batch, t_len, d)[:, nctx:, :]
```

```python
import functools

import jax
import jax.numpy as jnp
import numpy as np
from jax import lax
from jax.experimental import pallas as pl
from jax.experimental.pallas import tpu as pltpu

F32 = jnp.float32
BF16 = jnp.bfloat16

D_MODEL = 1024
NORM_EPS = 1e-6
N_BRANCHES = 3
GRID_W = 64

GLA_HEADS = 4
GLA_DK = D_MODEL // (2 * GLA_HEADS)
GLA_DV = D_MODEL // GLA_HEADS
GLA_QK = GLA_HEADS * GLA_DK
GLA_V = GLA_HEADS * GLA_DV
GLA_RANK = 16
GLA_TEMP = 16.0
GLA_CHUNK = 64
GLA_PHASE_LAG = 3

HEAD_DIM = 128
ATT_Q_HEADS = D_MODEL // HEAD_DIM
ATT_KV_HEADS = ATT_Q_HEADS // 4
ATT_GROUP = ATT_Q_HEADS // ATT_KV_HEADS
ATT_Q = ATT_Q_HEADS * HEAD_DIM
ATT_KV = ATT_KV_HEADS * HEAD_DIM
ROPE_THETA = 10000.0
LOG2_E = 1.4426950408889634

LRU_WIDTH = D_MODEL
LRU_BLOCKS = 8
LRU_BW = LRU_WIDTH // LRU_BLOCKS
LRU_C = 8.0
LRU_CONV = 4
LRU_CONV_LEFT = 2
LRU_SEG = 4

D_FF = 2816
FFN_CONV = 3

IN_SPLITS = (GLA_QK, GLA_QK, GLA_V, GLA_V, GLA_RANK, GLA_RANK,
             ATT_Q, ATT_KV, ATT_KV, LRU_WIDTH, LRU_WIDTH, N_BRANCHES * D_MODEL)

SUBLANES = 8
LANES = 128
TIME_BLOCK = 256

COL_GQ = 0
COL_GK = 512
COL_GV = 1024
COL_GR = 2048
COL_AQ = 3072
COL_LX = 4096
COL_LY = 5120
COL_GATE = 6144
COL_AK = 9216
COL_AV = 9472
COL_LR = 9728
D_IN_PAD = 10240

VMEM_LIMIT = 56 * 1024 * 1024


def _cparams(sem):
    return pltpu.CompilerParams(dimension_semantics=sem, vmem_limit_bytes=VMEM_LIMIT)


def _sigmoid(x):
    return 0.5 * jnp.tanh(0.5 * x) + 0.5


def _silu(x):
    return x * _sigmoid(x)


def _gelu_tanh(x):
    return x * (0.5 * (1.0 + jnp.tanh(0.7978845608028654 * (x + 0.044715 * (x * x * x)))))


def _split_bf16(x):
    hi = x.astype(BF16)
    lo = (x - hi.astype(F32)).astype(BF16)
    return hi, lo


def _rms(x, g):
    ms = jnp.mean(x * x, axis=-1, keepdims=True)
    return x * lax.rsqrt(ms + NORM_EPS) * g


def _ada_kernel(c_ref, w_ref, b_ref, o_ref):
    c = c_ref[...]
    o_ref[0] = jnp.dot(_silu(c), w_ref[0], preferred_element_type=F32,
                       precision=lax.Precision.HIGHEST) + b_ref[0]


def _ada_call(cvec, w_ada, b_ada):
    depth, d, n = w_ada.shape
    tn = 1536
    rows = cvec.shape[0]
    return pl.pallas_call(
        _ada_kernel,
        grid=(depth, n // tn),
        in_specs=[pl.BlockSpec((rows, d), lambda l, j: (0, 0)),
                  pl.BlockSpec((1, d, tn), lambda l, j: (l, 0, j)),
                  pl.BlockSpec((1, 1, tn), lambda l, j: (l, 0, j))],
        out_specs=pl.BlockSpec((1, rows, tn), lambda l, j: (l, 0, j)),
        out_shape=jax.ShapeDtypeStruct((depth, rows, n), F32),
        compiler_params=_cparams(("parallel", "parallel")),
        name="ada_mod",
    )(cvec, w_ada, b_ada.reshape(depth, 1, n))


def _nmm_kernel(x_ref, g_ref, mod_ref, w_ref, o_ref, h_ref, *, shift_idx, scale_idx, tm, nctx, tiles_per_batch):
    i = pl.program_id(0)

    @pl.when(pl.program_id(1) == 0)
    def _():
        y = _rms(x_ref[...], g_ref[...])
        row = (i % tiles_per_batch) * tm + lax.broadcasted_iota(jnp.int32, (tm, 1), 0)
        is_ctx = row < nctx
        shift = jnp.where(is_ctx, mod_ref[0, shift_idx:shift_idx + 1, :], mod_ref[0, 6 + shift_idx:7 + shift_idx, :])
        scale = jnp.where(is_ctx, mod_ref[0, scale_idx:scale_idx + 1, :], mod_ref[0, 6 + scale_idx:7 + scale_idx, :])
        h_ref[...] = (y * (1.0 + scale) + shift).astype(BF16)

    o_ref[...] = jnp.dot(h_ref[...], w_ref[...], preferred_element_type=F32).astype(o_ref.dtype)


def _nmm_call(x, g, modsel, w, *, shift_idx, scale_idx, tm, tn, t_len, nctx, name):
    m, d = x.shape
    n = w.shape[1]
    tiles_per_batch = t_len // tm
    kern = functools.partial(_nmm_kernel, shift_idx=shift_idx, scale_idx=scale_idx, tm=tm, nctx=nctx,
                             tiles_per_batch=tiles_per_batch)
    return pl.pallas_call(
        kern,
        grid=(m // tm, n // tn),
        in_specs=[pl.BlockSpec((tm, d), lambda i, j: (i, 0)),
                  pl.BlockSpec((1, d), lambda i, j: (0, 0)),
                  pl.BlockSpec((1, 12, d), lambda i, j: (i // tiles_per_batch, 0, 0)),
                  pl.BlockSpec((d, tn), lambda i, j: (0, j))],
        out_specs=pl.BlockSpec((tm, tn), lambda i, j: (i, j)),
        out_shape=jax.ShapeDtypeStruct((m, n), BF16),
        scratch_shapes=[pltpu.VMEM((tm, d), BF16)],
        compiler_params=_cparams(("parallel", "arbitrary")),
        name=name,
    )(x, g, modsel, w)


def _log_sigmoid(z):
    return jnp.minimum(z, 0.0) - jnp.log(1.0 + jnp.exp(-jnp.abs(z)))


def _gla_kernel(qf_ref, kf_ref, vf_ref, lrf_ref, qb_ref, kb_ref, vb_ref, lrb_ref, wd_ref, bd_ref,
                of_ref, ob_ref, stf_ref, stb_ref, *, tb):
    @pl.when(pl.program_id(1) == 0)
    def _():
        stf_ref[...] = jnp.zeros_like(stf_ref)
        stb_ref[...] = jnp.zeros_like(stb_ref)

    fwd = _gla_direction(qf_ref, kf_ref, vf_ref, lrf_ref, wd_ref, bd_ref, of_ref, stf_ref, reverse=False, tb=tb)
    bwd = _gla_direction(qb_ref, kb_ref, vb_ref, lrb_ref, wd_ref, bd_ref, ob_ref, stb_ref, reverse=True, tb=tb)
    done = object()
    running = True
    for _ in range(GLA_PHASE_LAG):
        running = next(fwd, done) is not done
    while running:
        step_f = next(fwd, done)
        step_b = next(bwd, done)
        running = step_f is not done or step_b is not done


def _gla_direction(q_ref, k_ref, v_ref, lr_ref, wd_ref, bd_ref, o_ref, st_ref, *, reverse, tb):
    c = GLA_CHUNK
    nchunk = tb // c
    d = 1 if reverse else 0

    lr = lr_ref[...]
    z = (jnp.dot(lr, wd_ref[d, 0], preferred_element_type=F32)
         + jnp.dot(lr, wd_ref[d, 1], preferred_element_type=F32) + bd_ref[d])
    yield
    log_a = _log_sigmoid(z) * (1.0 / GLA_TEMP)
    r = lax.broadcasted_iota(jnp.int32, (tb, tb), 0)
    s = lax.broadcasted_iota(jnp.int32, (tb, tb), 1)
    keep = ((r // c) == (s // c)) & ((s >= r) if reverse else (s <= r))
    tri = jnp.where(keep, 1.0, 0.0).astype(BF16)
    la_hi, la_lo = _split_bf16(log_a)
    cum = jnp.dot(tri, la_hi, preferred_element_type=F32) + jnp.dot(tri, la_lo, preferred_element_type=F32)
    yield

    last = [(ci * c) if reverse else (ci * c + c - 1) for ci in range(nchunk)]
    tots = [cum[i:i + 1] for i in last]
    tot_rows = jnp.concatenate([jnp.broadcast_to(tt, (c, GLA_QK)) for tt in tots], axis=0)
    e_cum = jnp.exp(cum)
    e_inv = jnp.exp(-cum)
    e_end = jnp.exp(tot_rows - cum)
    e_tot = [jnp.exp(tt) for tt in tots]

    order = range(nchunk - 1, -1, -1) if reverse else range(nchunk)
    for h in range(GLA_HEADS):
        yield
        hs = slice(h * GLA_DK, (h + 1) * GLA_DK)
        vs = slice(h * GLA_DV, (h + 1) * GLA_DV)
        q = q_ref[:, hs].astype(F32) * (GLA_DK ** -0.5)
        k = k_ref[:, hs].astype(F32)
        v = v_ref[:, vs]
        qd = (q * e_cum[:, hs]).astype(BF16)
        ki = (k * e_inv[:, hs]).astype(BF16)
        ke = (k * e_end[:, hs]).astype(BF16)
        att = lax.dot_general(qd, ki, NT_DIMS, preferred_element_type=F32)
        o_intra = jnp.dot(jnp.where(keep, att, 0.0).astype(BF16), v, preferred_element_type=F32)
        st = st_ref[h]
        o_inter = [None] * nchunk
        for ci in order:
            rows = slice(ci * c, (ci + 1) * c)
            o_inter[ci] = lax.dot_general(qd[rows], st.astype(BF16), NT_DIMS, preferred_element_type=F32)
            upd = lax.dot_general(v[rows], ke[rows], (((0,), (0,)), ((), ())), preferred_element_type=F32)
            st = st * e_tot[ci][:, hs] + upd
        st_ref[h] = st
        o_ref[:, vs] = (o_intra + jnp.concatenate(o_inter, axis=0)).astype(o_ref.dtype)


def _time_block_index(t, nblk, reverse):
    if not reverse:
        return t
    return jnp.where(t == 0, 0, nblk - t)


def _gla_call(p, wd_pad, bd, *, batch, t_len):
    m = p.shape[0]
    tb = TIME_BLOCK
    nblk = t_len // tb

    def blocks(reverse):
        def rowblk(b, t):
            return b * nblk + _time_block_index(t, nblk, reverse)
        return [pl.BlockSpec((tb, GLA_QK), lambda b, t: (rowblk(b, t), COL_GQ // GLA_QK)),
                pl.BlockSpec((tb, GLA_QK), lambda b, t: (rowblk(b, t), COL_GK // GLA_QK)),
                pl.BlockSpec((tb, GLA_V), lambda b, t: (rowblk(b, t), COL_GV // GLA_V)),
                pl.BlockSpec((tb, LANES), lambda b, t: (rowblk(b, t), COL_LR // LANES))], \
            pl.BlockSpec((tb, GLA_V), lambda b, t: (rowblk(b, t), 0))

    in_f, out_f = blocks(False)
    in_b, out_b = blocks(True)
    state = pltpu.VMEM((GLA_HEADS, GLA_DV, GLA_DK), F32)
    return pl.pallas_call(
        functools.partial(_gla_kernel, tb=tb),
        grid=(batch, nblk),
        in_specs=in_f + in_b + [pl.BlockSpec((2, 2, LANES, GLA_QK), lambda b, t: (0, 0, 0, 0)),
                                pl.BlockSpec((2, 1, GLA_QK), lambda b, t: (0, 0, 0))],
        out_specs=[out_f, out_b],
        out_shape=[jax.ShapeDtypeStruct((m, GLA_V), BF16)] * 2,
        scratch_shapes=[state, state],
        compiler_params=_cparams(("parallel", "arbitrary")),
        name="gla_scan",
    )(p, p, p, p, p, p, p, p, wd_pad, bd)


def _qkprep_kernel(q_ref, k_ref, v_ref, cos_ref, sin_ref, gq_ref, gk_ref, qo_ref, ko_ref, vo_ref):
    cos = cos_ref[...]
    sin = sin_ref[...]
    lane = lax.broadcasted_iota(jnp.int32, (1, HEAD_DIM), 1)
    first = (lane % (HEAD_DIM // 2)) < (HEAD_DIM // 4)

    def prep(x, g, scale):
        y = _rms(x.astype(F32), g)
        partner = jnp.where(first, pltpu.roll(y, HEAD_DIM - HEAD_DIM // 4, 1), pltpu.roll(y, HEAD_DIM // 4, 1))
        out = y * cos + partner * sin
        if scale != 1.0:
            out = out * scale
        return out.astype(BF16)

    for h in range(ATT_Q_HEADS):
        hs = slice(h * HEAD_DIM, (h + 1) * HEAD_DIM)
        qo_ref[:, hs] = prep(q_ref[:, hs], gq_ref[...], HEAD_DIM ** -0.5 * LOG2_E)
    for h in range(ATT_KV_HEADS):
        hs = slice(h * HEAD_DIM, (h + 1) * HEAD_DIM)
        ko_ref[:, hs] = prep(k_ref[:, hs], gk_ref[...], 1.0)
        vo_ref[:, 2 * h * HEAD_DIM:(2 * h + 1) * HEAD_DIM] = v_ref[:, hs]
        vo_ref[:, (2 * h + 1) * HEAD_DIM:(2 * h + 2) * HEAD_DIM] = jnp.ones((v_ref.shape[0], HEAD_DIM), BF16)


def _qkprep_call(p, cos_t, sin_t, gq, gk, *, t_len):
    m = p.shape[0]
    tb = TIME_BLOCK
    nblk = t_len // tb
    return pl.pallas_call(
        _qkprep_kernel,
        grid=(m // tb,),
        in_specs=[pl.BlockSpec((tb, ATT_Q), lambda i: (i, COL_AQ // ATT_Q)),
                  pl.BlockSpec((tb, ATT_KV), lambda i: (i, COL_AK // ATT_KV)),
                  pl.BlockSpec((tb, ATT_KV), lambda i: (i, COL_AV // ATT_KV)),
                  pl.BlockSpec((tb, HEAD_DIM), lambda i: (i % nblk, 0)),
                  pl.BlockSpec((tb, HEAD_DIM), lambda i: (i % nblk, 0)),
                  pl.BlockSpec((1, HEAD_DIM), lambda i: (0, 0)),
                  pl.BlockSpec((1, HEAD_DIM), lambda i: (0, 0))],
        out_specs=[pl.BlockSpec((tb, ATT_Q), lambda i: (i, 0)),
                   pl.BlockSpec((tb, ATT_KV), lambda i: (i, 0)),
                   pl.BlockSpec((tb, 2 * ATT_KV), lambda i: (i, 0))],
        out_shape=[jax.ShapeDtypeStruct((m, ATT_Q), BF16), jax.ShapeDtypeStruct((m, ATT_KV), BF16),
                   jax.ShapeDtypeStruct((m, 2 * ATT_KV), BF16)],
        compiler_params=_cparams(("parallel",)),
        name="qk_prep",
    )(p, p, p, cos_t, sin_t, gq, gk)


NEG_BIG = -1e30


ATT_ROW_SUB = 64
ATT_HEADS_PER_STEP = 2
NT_DIMS = (((1,), (1,)), ((), ()))


def _attn_kernel(q_ref, k_ref, v_ref, o_ref, s_ref, p_ref, a_ref, m_ref, acc_ref, *, tq, tk, t_len):
    nkv = t_len // tk
    sub = ATT_ROW_SUB
    nlt = tk // LANES
    nh = q_ref.shape[1] // HEAD_DIM
    m_ref[...] = jnp.full_like(m_ref, NEG_BIG)
    acc_ref[...] = jnp.zeros_like(acc_ref)
    items = [(h, j) for h in range(nh) for j in range(nkv)]

    def scores(i):
        h, j = items[i]
        q = q_ref[:, h * HEAD_DIM:(h + 1) * HEAD_DIM]
        s_ref[i % 2] = lax.dot_general(q, k_ref[j * tk:(j + 1) * tk, :], NT_DIMS, preferred_element_type=F32)

    def exponentials(i):
        h, _ = items[i]
        slot = i % 2
        for r in range(tq // sub):
            rows = slice(r * sub, (r + 1) * sub)
            tiles = [s_ref[slot, rows, t * LANES:(t + 1) * LANES] for t in range(nlt)]
            mx = tiles[0]
            for t in range(1, nlt):
                mx = jnp.maximum(mx, tiles[t])
            m_prev = m_ref[h, rows, :]
            m_new = jnp.maximum(m_prev, jnp.max(mx, axis=1, keepdims=True))
            a_ref[slot, rows, :] = jnp.exp2(m_prev - m_new)
            m_ref[h, rows, :] = m_new
            for t in range(nlt):
                p_ref[slot, rows, t * LANES:(t + 1) * LANES] = jnp.exp2(tiles[t] - m_new).astype(BF16)

    def weighted_values(i):
        h, j = items[i]
        slot = i % 2
        pv = jnp.dot(p_ref[slot], v_ref[j * tk:(j + 1) * tk, :], preferred_element_type=F32)
        alpha = a_ref[slot]
        acc_ref[h] = jnp.concatenate([alpha, alpha], axis=1) * acc_ref[h] + pv

    n = len(items)
    for step in range(n + 2):
        if step < n:
            scores(step)
        if 1 <= step <= n:
            exponentials(step - 1)
        if step >= 2:
            weighted_values(step - 2)

    for h in range(nh):
        acc = acc_ref[h]
        o_ref[:, h * HEAD_DIM:(h + 1) * HEAD_DIM] = (acc[:, :HEAD_DIM] / acc[:, HEAD_DIM:]).astype(o_ref.dtype)


def _attn_call(qn, kn, vext, *, batch, t_len, tq):
    m = qn.shape[0]
    nq = t_len // tq
    tk = tq
    assert tq % ATT_ROW_SUB == 0 and tk % LANES == 0
    kern = functools.partial(_attn_kernel, tq=tq, tk=tk, t_len=t_len)
    nh = ATT_HEADS_PER_STEP
    hsteps = ATT_GROUP // nh
    width = nh * HEAD_DIM
    return pl.pallas_call(
        kern,
        grid=(batch, ATT_KV_HEADS, nq, hsteps),
        in_specs=[pl.BlockSpec((tq, width), lambda b, g, i, h: (b * nq + i, g * hsteps + h)),
                  pl.BlockSpec((t_len, HEAD_DIM), lambda b, g, i, h: (b, g)),
                  pl.BlockSpec((t_len, 2 * HEAD_DIM), lambda b, g, i, h: (b, g))],
        out_specs=pl.BlockSpec((tq, width), lambda b, g, i, h: (b * nq + i, g * hsteps + h)),
        out_shape=jax.ShapeDtypeStruct((m, ATT_Q), BF16),
        scratch_shapes=[pltpu.VMEM((2, tq, tk), F32), pltpu.VMEM((2, tq, tk), BF16),
                        pltpu.VMEM((2, tq, LANES), F32), pltpu.VMEM((nh, tq, LANES), F32),
                        pltpu.VMEM((nh, tq, 2 * HEAD_DIM), F32)],
        compiler_params=_cparams(("parallel", "parallel", "arbitrary", "arbitrary")),
        name="flash_attn",
    )(qn, kn, vext)


def _attn_ctx_kernel(q_ref, k_ref, v_ref, att_ref, o_ref):
    del att_ref
    k = k_ref[...]
    v = v_ref[...]
    for h in range(ATT_GROUP):
        hs = slice(h * HEAD_DIM, (h + 1) * HEAD_DIM)
        s = lax.dot_general(q_ref[:, hs], k, NT_DIMS, preferred_element_type=F32)
        pexp = jnp.exp2(s - jnp.max(s, axis=1, keepdims=True)).astype(BF16)
        acc = jnp.dot(pexp, v, preferred_element_type=F32)
        o_ref[:, hs] = (acc[:, :HEAD_DIM] / acc[:, HEAD_DIM:]).astype(o_ref.dtype)


def _attn_ctx_call(qn, kn, vext, att, *, batch, t_len, nctx):
    nblk = t_len // nctx
    width = ATT_GROUP * HEAD_DIM
    return pl.pallas_call(
        _attn_ctx_kernel,
        grid=(batch, ATT_KV_HEADS),
        in_specs=[pl.BlockSpec((nctx, width), lambda b, g: (b * nblk, g)),
                  pl.BlockSpec((nctx, HEAD_DIM), lambda b, g: (b * nblk, g)),
                  pl.BlockSpec((nctx, 2 * HEAD_DIM), lambda b, g: (b * nblk, g)),
                  pl.BlockSpec(memory_space=pl.ANY)],
        out_specs=pl.BlockSpec((nctx, width), lambda b, g: (b * nblk, g)),
        out_shape=jax.ShapeDtypeStruct(att.shape, att.dtype),
        input_output_aliases={3: 0},
        compiler_params=_cparams(("parallel", "parallel")),
        name="ctx_attn",
    )(qn, kn, vext, att)


def _halo_valid(blk, nblk):
    prev_ok = blk >= 2
    next_ok = jnp.logical_and(blk != 0, blk != nblk - 1)
    return prev_ok, next_ok


def _shift_matrices(tb, offsets):
    r = jnp.arange(tb)[:, None]
    c = jnp.arange(tb)[None, :]
    return jnp.stack([(c == r + off) for off in offsets]).astype(BF16)


def _lru_kernel(x_ref, xp_ref, xn_ref, sh_ref, cw_ref, cb_ref, wg_ref, bg_ref, lam_ref, h_ref,
                carry_ref, a_ref, u_ref, hs_ref, hl_ref, ac_ref, *, reverse, tb, nblk):
    t = pl.program_id(1)
    blk = _time_block_index(t, nblk, reverse)

    @pl.when(t == 0)
    def _():
        carry_ref[...] = jnp.zeros_like(carry_ref)

    prev_ok, next_ok = _halo_valid(blk, nblk)
    xb = x_ref[...]
    x = xb.astype(F32)
    prev = jnp.where(prev_ok, xp_ref[...].astype(F32), 0.0)
    nxt = jnp.where(next_ok, xn_ref[...].astype(F32), 0.0)
    xm2 = jnp.dot(sh_ref[0], xb, preferred_element_type=F32)
    xm1 = jnp.dot(sh_ref[1], xb, preferred_element_type=F32)
    xp1 = jnp.dot(sh_ref[2], xb, preferred_element_type=F32)
    xc = cw_ref[0:1] * xm2 + cw_ref[1:2] * xm1 + cw_ref[2:3] * x + cw_ref[3:4] * xp1 + cb_ref[...]
    r8 = lax.broadcasted_iota(jnp.int32, (SUBLANES, 1), 0)
    p6 = prev[SUBLANES - 2:SUBLANES - 1]
    p7 = prev[SUBLANES - 1:SUBLANES]
    fix_first = jnp.where(r8 == 0, cw_ref[0:1] * p6 + cw_ref[1:2] * p7, jnp.where(r8 == 1, cw_ref[0:1] * p7, 0.0))
    fix_last = jnp.where(r8 == SUBLANES - 1, cw_ref[3:4] * nxt[0:1], 0.0)
    xc = jnp.concatenate([xc[:SUBLANES] + fix_first, xc[SUBLANES:tb - SUBLANES], xc[tb - SUBLANES:] + fix_last],
                         axis=0)

    xcb = xc.astype(BF16)
    zs = []
    for gi in range(2):
        parts = [jnp.dot(xcb[:, hb * LRU_BW:(hb + 1) * LRU_BW], wg_ref[0, gi, hb], preferred_element_type=F32)
                 for hb in range(LRU_BLOCKS)]
        zs.append(jnp.concatenate(parts, axis=-1) + bg_ref[0, gi:gi + 1, :])
    gate_r = _sigmoid(zs[0])
    gate_i = _sigmoid(zs[1])
    neg_lam = -lam_ref[0]
    softplus = jnp.maximum(neg_lam, 0.0) + jnp.log(1.0 + jnp.exp(-jnp.abs(neg_lam)))
    log_a = (-LRU_C) * gate_r * softplus
    a = jnp.exp(log_a)
    one_m_a2 = jnp.tanh(-log_a) * (1.0 + a * a)
    root = jnp.where(one_m_a2 > 0.0, one_m_a2 * lax.rsqrt(one_m_a2), 0.0)
    u = root * (gate_i * xc)

    ncol = LRU_WIDTH // LANES
    for cb in range(ncol):
        a_ref[cb] = a[:, cb * LANES:(cb + 1) * LANES]
        u_ref[cb] = u[:, cb * LANES:(cb + 1) * LANES]
    seg = LRU_SEG
    grp_rows = SUBLANES * seg
    ngrp = tb // grp_rows

    def tile_rows(grp, j):
        return pl.ds(grp * grp_rows + j, SUBLANES, stride=seg)

    def strided(ref, grp, j):
        return jnp.concatenate([ref[cb, tile_rows(grp, j), :] for cb in range(ncol)], axis=1)

    steps = range(seg - 1, -1, -1) if reverse else range(seg)
    summaries = []
    for grp in range(ngrp):
        h_loc = jnp.zeros((SUBLANES, LRU_WIDTH), F32)
        a_cum = jnp.ones((SUBLANES, LRU_WIDTH), F32)
        for j in steps:
            a_j = strided(a_ref, grp, j)
            u_j = strided(u_ref, grp, j)
            h_loc = a_j * h_loc + u_j
            a_cum = a_j * a_cum
            rows = slice((grp * seg + j) * SUBLANES, (grp * seg + j + 1) * SUBLANES)
            hl_ref[rows, :] = h_loc
            ac_ref[rows, :] = a_cum
        e, pr = h_loc, a_cum
        for sft in (1, 2, 4):
            if reverse:
                e_sh, p_sh = pltpu.roll(e, SUBLANES - sft, 0), pltpu.roll(pr, SUBLANES - sft, 0)
                valid = r8 < SUBLANES - sft
            else:
                e_sh, p_sh, valid = pltpu.roll(e, sft, 0), pltpu.roll(pr, sft, 0), r8 >= sft
            e = jnp.where(valid, e + pr * e_sh, e)
            pr = jnp.where(valid, pr * p_sh, pr)
        summaries.append((e, pr))

    c0 = carry_ref[...]
    c_in = [None] * ngrp
    for grp in (range(ngrp - 1, -1, -1) if reverse else range(ngrp)):
        e, pr = summaries[grp]
        g = e + pr * c0
        if reverse:
            c_in[grp] = jnp.where(r8 == SUBLANES - 1, c0, pltpu.roll(g, SUBLANES - 1, 0))
            c0 = jnp.broadcast_to(g[0:1], (SUBLANES, LRU_WIDTH))
        else:
            c_in[grp] = jnp.where(r8 == 0, c0, pltpu.roll(g, 1, 0))
            c0 = jnp.broadcast_to(g[SUBLANES - 1:SUBLANES], (SUBLANES, LRU_WIDTH))
    carry_ref[...] = c0

    for grp in range(ngrp):
        for j in range(seg):
            rows = slice((grp * seg + j) * SUBLANES, (grp * seg + j + 1) * SUBLANES)
            h_j = hl_ref[rows, :] + ac_ref[rows, :] * c_in[grp]
            for cb in range(ncol):
                hs_ref[cb, tile_rows(grp, j), :] = h_j[:, cb * LANES:(cb + 1) * LANES]
    for cb in range(ncol):
        h_ref[:, cb * LANES:(cb + 1) * LANES] = hs_ref[cb]


def _lru_call(p, cw, cb, wg, bg, lam, *, reverse, batch, t_len):
    m = p.shape[0]
    tb = TIME_BLOCK
    nblk = t_len // tb
    per = tb // SUBLANES
    nrow8 = m // SUBLANES
    d = 1 if reverse else 0

    def rowblk(b, t):
        return b * nblk + _time_block_index(t, nblk, reverse)

    kern = functools.partial(_lru_kernel, reverse=reverse, tb=tb, nblk=nblk)
    return pl.pallas_call(
        kern,
        grid=(batch, nblk),
        in_specs=[pl.BlockSpec((tb, LRU_WIDTH), lambda b, t: (rowblk(b, t), COL_LX // LRU_WIDTH)),
                  pl.BlockSpec((SUBLANES, LRU_WIDTH),
                               lambda b, t: (jnp.maximum(rowblk(b, t) * per - 1, 0), COL_LX // LRU_WIDTH)),
                  pl.BlockSpec((SUBLANES, LRU_WIDTH),
                               lambda b, t: (jnp.minimum((rowblk(b, t) + 1) * per, nrow8 - 1), COL_LX // LRU_WIDTH)),
                  pl.BlockSpec((LRU_CONV - 1, tb, tb), lambda b, t: (0, 0, 0)),
                  pl.BlockSpec((LRU_CONV, LRU_WIDTH), lambda b, t: (0, 0)),
                  pl.BlockSpec((1, LRU_WIDTH), lambda b, t: (0, 0)),
                  pl.BlockSpec((1, 2, LRU_BLOCKS, LRU_BW, LRU_BW), lambda b, t: (d, 0, 0, 0, 0)),
                  pl.BlockSpec((1, 2, LRU_WIDTH), lambda b, t: (d, 0, 0)),
                  pl.BlockSpec((1, 1, LRU_WIDTH), lambda b, t: (d, 0, 0))],
        out_specs=pl.BlockSpec((tb, LRU_WIDTH), lambda b, t: (rowblk(b, t), 0)),
        out_shape=jax.ShapeDtypeStruct((m, LRU_WIDTH), F32),
        scratch_shapes=([pltpu.VMEM((SUBLANES, LRU_WIDTH), F32)]
                        + [pltpu.VMEM((LRU_WIDTH // LANES, tb, LANES), F32)] * 3
                        + [pltpu.VMEM((tb, LRU_WIDTH), F32)] * 2),
        compiler_params=_cparams(("parallel", "arbitrary")),
        name="lru_bwd" if reverse else "lru_fwd",
    )(p, p, p, _shift_matrices(tb, (-2, -1, 1)), cw, cb, wg, bg, lam)


def _select_mod(mod_ref, idx, i, tm, tiles_per_batch, nctx):
    row = (i % tiles_per_batch) * tm + lax.broadcasted_iota(jnp.int32, (tm, 1), 0)
    return jnp.where(row < nctx, mod_ref[0, idx:idx + 1, :], mod_ref[0, 6 + idx:7 + idx, :])


def _merge_kernel(of_ref, ob_ref, r_ref, att_ref, hf_ref, hb_ref, y_ref, gate_ref, x_ref, mod_ref,
                  gn_ref, gp_ref, wb_ref, wo_ref, o_ref, *, tm, tiles_per_batch, nctx):
    i = pl.program_id(0)
    o = of_ref[...].astype(F32) + ob_ref[...].astype(F32)
    gla_parts = []
    for h in range(GLA_HEADS):
        vs = slice(h * GLA_DV, (h + 1) * GLA_DV)
        gla_parts.append(_rms(o[:, vs], gn_ref[:, vs]))
    gla = jnp.concatenate(gla_parts, axis=-1) * _silu(r_ref[...].astype(F32))
    lru = (hf_ref[...] + hb_ref[...]) * _gelu_tanh(y_ref[...].astype(F32))
    branches = (gla.astype(BF16), att_ref[...], lru.astype(BF16))
    mixed = None
    for bi in range(N_BRANCHES):
        gate = _sigmoid(gate_ref[:, bi * D_MODEL:(bi + 1) * D_MODEL].astype(F32))
        term = gate * jnp.dot(branches[bi], wb_ref[bi], preferred_element_type=F32)
        mixed = term if mixed is None else mixed + term
    y = jnp.dot(mixed.astype(BF16), wo_ref[...], preferred_element_type=F32)
    gate_vec = _select_mod(mod_ref, 2, i, tm, tiles_per_batch, nctx)
    o_ref[...] = x_ref[...] + gate_vec * _rms(y, gp_ref[...])


def _merge_call(o_f, o_b, p, att, h_f, h_b, x, modsel, gn, gp, wb, wo, *, tm, t_len, nctx):
    m, d = x.shape
    tiles_per_batch = t_len // tm
    kern = functools.partial(_merge_kernel, tm=tm, tiles_per_batch=tiles_per_batch, nctx=nctx)
    row = lambda i: (i, 0)
    const2 = lambda i: (0, 0)
    return pl.pallas_call(
        kern,
        grid=(m // tm,),
        in_specs=[pl.BlockSpec((tm, d), row),
                  pl.BlockSpec((tm, d), row),
                  pl.BlockSpec((tm, d), lambda i: (i, COL_GR // D_MODEL)),
                  pl.BlockSpec((tm, d), row),
                  pl.BlockSpec((tm, d), row),
                  pl.BlockSpec((tm, d), row),
                  pl.BlockSpec((tm, d), lambda i: (i, COL_LY // D_MODEL)),
                  pl.BlockSpec((tm, N_BRANCHES * d), lambda i: (i, COL_GATE // (N_BRANCHES * D_MODEL))),
                  pl.BlockSpec((tm, d), row),
                  pl.BlockSpec((1, 12, d), lambda i: (i // tiles_per_batch, 0, 0)),
                  pl.BlockSpec((1, d), const2),
                  pl.BlockSpec((1, d), const2),
                  pl.BlockSpec((N_BRANCHES, d, d), lambda i: (0, 0, 0)),
                  pl.BlockSpec((d, d), const2)],
        out_specs=pl.BlockSpec((tm, d), row),
        out_shape=jax.ShapeDtypeStruct((m, d), F32),
        compiler_params=_cparams(("parallel",)),
        name="merge_out",
    )(o_f, o_b, p, att, h_f, h_b, p, p, x, modsel, gn, gp, wb, wo)


def _ffn_down_kernel(u_ref, up_ref, un_ref, cw_ref, cb_ref, wd_ref, x_ref, mod_ref, gp_ref, o_ref,
                     *, tb, nblk):
    i = pl.program_id(0)
    blk = i % nblk
    prev_ok, next_ok = _halo_valid(blk, nblk)
    u = u_ref[...].astype(F32)
    prev = jnp.where(prev_ok, up_ref[SUBLANES - 1:SUBLANES, :].astype(F32), 0.0)
    nxt = jnp.where(next_ok, un_ref[0:1, :].astype(F32), 0.0)
    r8 = lax.broadcasted_iota(jnp.int32, (SUBLANES, 1), 0)
    um1 = pltpu.roll(u, 1, 0)
    up1 = pltpu.roll(u, tb - 1, 0)
    um1 = jnp.concatenate([jnp.where(r8 == 0, prev, um1[:SUBLANES]), um1[SUBLANES:]], axis=0)
    up1 = jnp.concatenate([up1[:tb - SUBLANES], jnp.where(r8 == SUBLANES - 1, nxt, up1[tb - SUBLANES:])], axis=0)
    uc = cw_ref[0:1] * um1 + cw_ref[1:2] * u + cw_ref[2:3] * up1 + cb_ref[...]
    act = (_silu(uc[:, :D_FF]) * uc[:, D_FF:]).astype(BF16)
    y = jnp.dot(act, wd_ref[...], preferred_element_type=F32)
    gate_vec = _select_mod(mod_ref, 5, i, tb, nblk, tb)
    o_ref[...] = x_ref[...] + gate_vec * _rms(y, gp_ref[...])


def _ffn_down_call(u, cw, cb, wd, x, modsel, gp, *, t_len):
    m, d = x.shape
    tb = TIME_BLOCK
    nblk = t_len // tb
    per = tb // SUBLANES
    nrow8 = m // SUBLANES
    kern = functools.partial(_ffn_down_kernel, tb=tb, nblk=nblk)
    return pl.pallas_call(
        kern,
        grid=(m // tb,),
        in_specs=[pl.BlockSpec((tb, 2 * D_FF), lambda i: (i, 0)),
                  pl.BlockSpec((SUBLANES, 2 * D_FF), lambda i: (jnp.maximum(i * per - 1, 0), 0)),
                  pl.BlockSpec((SUBLANES, 2 * D_FF), lambda i: (jnp.minimum((i + 1) * per, nrow8 - 1), 0)),
                  pl.BlockSpec((FFN_CONV, 2 * D_FF), lambda i: (0, 0)),
                  pl.BlockSpec((1, 2 * D_FF), lambda i: (0, 0)),
                  pl.BlockSpec((D_FF, d), lambda i: (0, 0)),
                  pl.BlockSpec((tb, d), lambda i: (i, 0)),
                  pl.BlockSpec((1, 12, d), lambda i: (i // nblk, 0, 0)),
                  pl.BlockSpec((1, d), lambda i: (0, 0))],
        out_specs=pl.BlockSpec((tb, d), lambda i: (i, 0)),
        out_shape=jax.ShapeDtypeStruct((m, d), F32),
        compiler_params=_cparams(("parallel",)),
        name="ffn_down",
    )(u, u, u, cw, cb, wd, x, modsel, gp)


def _rope_tables(seq, nctx):
    rows = seq // GRID_W
    row = np.repeat(np.arange(rows), GRID_W)
    col = np.tile(np.arange(GRID_W), rows)
    axis_dim = HEAD_DIM // 2
    inv_freq = (1.0 / (np.float32(ROPE_THETA) ** (np.arange(0, axis_dim, 2, dtype=np.float32) / np.float32(axis_dim))))
    ang = np.stack([row, col], axis=-1).astype(np.float32)[:, :, None] * inv_freq.astype(np.float32)
    cos, sin = np.cos(ang), np.sin(ang)
    cos_t = np.concatenate([cos[:, 0], cos[:, 0], cos[:, 1], cos[:, 1]], axis=-1)
    sin_t = np.concatenate([-sin[:, 0], sin[:, 0], -sin[:, 1], sin[:, 1]], axis=-1)
    cos_t = np.concatenate([np.ones((nctx, HEAD_DIM), np.float32), cos_t], axis=0)
    sin_t = np.concatenate([np.zeros((nctx, HEAD_DIM), np.float32), sin_t], axis=0)
    return jnp.asarray(cos_t, F32), jnp.asarray(sin_t, F32)


def _permute_w_in(w):
    offs = [0]
    for s in IN_SPLITS:
        offs.append(offs[-1] + s)
    seg = lambda i: w[:, offs[i]:offs[i + 1]]
    gq, gk, gv, gr, lrf, lrb, aq, ak, av, lx, ly, gates = (seg(i) for i in range(12))
    pad = jnp.zeros((w.shape[0], D_IN_PAD - COL_LR - 2 * GLA_RANK), w.dtype)
    return jnp.concatenate([gq, gk, gv, gr, aq, lx, ly, gates, ak, av, lrf, lrb, pad], axis=1).astype(BF16)


def _largest_tile(t_len, cap):
    best = SUBLANES
    for cand in range(SUBLANES, cap + 1, SUBLANES):
        if t_len % cand == 0:
            best = cand
    return best


def kernel(x, c, ctx, c_ctx, w_ada, b_ada, norm_gains, w_in, gla_w_decay, gla_b_decay, gla_norm_g,
           att_q_norm_g, att_k_norm_g, lru_conv_w, lru_conv_b, lru_w_gates, lru_b_gates, lru_lambda,
           w_branch, w_out, ffn_w_up, ffn_conv_w, ffn_conv_b, ffn_w_down):
    batch, seq, d = x.shape
    nctx = ctx.shape[1]
    depth = w_ada.shape[0]
    assert d == D_MODEL and nctx == TIME_BLOCK and seq % TIME_BLOCK == 0 and seq % GRID_W == 0
    t_len = nctx + seq
    m = batch * t_len
    tm_big = _largest_tile(t_len, 1408)
    tq = _largest_tile(t_len, 768)
    assert tm_big % TIME_BLOCK == 0 or tm_big % LANES == 0
    assert tq % TIME_BLOCK == 0

    xc = jnp.concatenate([ctx, x], axis=1).reshape(m, d)

    rows = -(-(batch + 1) // SUBLANES) * SUBLANES
    cvec = jnp.zeros((rows, d), F32).at[:batch].set(c).at[batch].set(c_ctx)
    mod = _ada_call(cvec, w_ada, b_ada)
    mod = mod.reshape(depth, rows, 6, d)
    cos_t, sin_t = _rope_tables(seq, nctx)

    for l in range(depth):
        mod_c = jnp.broadcast_to(mod[l, batch][None], (batch, 6, d))
        modsel = jnp.concatenate([mod_c, mod[l, :batch]], axis=1)
        g_pre_mix, g_post_mix, g_pre_ffn, g_post_ffn = (norm_gains[l, i].reshape(1, d) for i in range(4))

        p = _nmm_call(xc, g_pre_mix, modsel, _permute_w_in(w_in[l]), shift_idx=0, scale_idx=1,
                      tm=tm_big, tn=1024, t_len=t_len, nctx=nctx, name="in_proj")

        wd_pad = jnp.zeros((2, LANES, GLA_QK), F32)
        wd_pad = wd_pad.at[0, :GLA_RANK].set(gla_w_decay[l, 0]).at[1, GLA_RANK:2 * GLA_RANK].set(gla_w_decay[l, 1])
        wd_pad = jnp.stack(_split_bf16(wd_pad), axis=1)
        bd = gla_b_decay[l].reshape(2, 1, GLA_QK)
        o_f, o_b = _gla_call(p, wd_pad, bd, batch=batch, t_len=t_len)

        qn, kn, vext = _qkprep_call(p, cos_t, sin_t, att_q_norm_g[l].reshape(1, HEAD_DIM),
                                    att_k_norm_g[l].reshape(1, HEAD_DIM), t_len=t_len)
        att = _attn_call(qn, kn, vext, batch=batch, t_len=t_len, tq=tq)
        if l < depth - 1:
            att = _attn_ctx_call(qn, kn, vext, att, batch=batch, t_len=t_len, nctx=nctx)

        wg = lru_w_gates[l].astype(BF16)
        lam = lru_lambda[l].reshape(2, 1, LRU_WIDTH)
        cb = lru_conv_b[l].reshape(1, LRU_WIDTH)
        h_f = _lru_call(p, lru_conv_w[l], cb, wg, lru_b_gates[l], lam, reverse=False, batch=batch, t_len=t_len)
        h_b = _lru_call(p, lru_conv_w[l], cb, wg, lru_b_gates[l], lam, reverse=True, batch=batch, t_len=t_len)

        xc = _merge_call(o_f, o_b, p, att, h_f, h_b, xc, modsel, gla_norm_g[l].reshape(1, GLA_V), g_post_mix,
                         w_branch[l].astype(BF16), w_out[l].astype(BF16), tm=TIME_BLOCK, t_len=t_len, nctx=nctx)

        u = _nmm_call(xc, g_pre_ffn, modsel, ffn_w_up[l].astype(BF16), shift_idx=3, scale_idx=4,
                      tm=tm_big, tn=_largest_tile(2 * D_FF, 1408), t_len=t_len, nctx=nctx, name="ffn_up")
        xc = _ffn_down_call(u, ffn_conv_w[l], ffn_conv_b[l].reshape(1, 2 * D_FF), ffn_w_down[l].astype(BF16),
                            xc, modsel, g_post_ffn, t_len=t_len)

    return xc.reshape(batch, t_len, d)[:, nctx:, :]
```

```python
import functools

import jax
import jax.numpy as jnp
import numpy as np
from jax import lax
from jax.experimental import pallas as pl
from jax.experimental.pallas import tpu as pltpu

F32 = jnp.float32
BF16 = jnp.bfloat16

D_MODEL = 1024
NORM_EPS = 1e-6
N_BRANCHES = 3
GRID_W = 64

GLA_HEADS = 4
GLA_DK = D_MODEL // (2 * GLA_HEADS)
GLA_DV = D_MODEL // GLA_HEADS
GLA_QK = GLA_HEADS * GLA_DK
GLA_V = GLA_HEADS * GLA_DV
GLA_RANK = 16
GLA_TEMP = 16.0
GLA_CHUNK = 64
GLA_PHASE_LAG = 3

HEAD_DIM = 128
ATT_Q_HEADS = D_MODEL // HEAD_DIM
ATT_KV_HEADS = ATT_Q_HEADS // 4
ATT_GROUP = ATT_Q_HEADS // ATT_KV_HEADS
ATT_Q = ATT_Q_HEADS * HEAD_DIM
ATT_KV = ATT_KV_HEADS * HEAD_DIM
ROPE_THETA = 10000.0
LOG2_E = 1.4426950408889634

LRU_WIDTH = D_MODEL
LRU_BLOCKS = 8
LRU_BW = LRU_WIDTH // LRU_BLOCKS
LRU_C = 8.0
LRU_CONV = 4
LRU_CONV_LEFT = 2
LRU_SEG = 4

D_FF = 2816
FFN_CONV = 3
FFN_COL_CHUNK = 256

IN_SPLITS = (GLA_QK, GLA_QK, GLA_V, GLA_V, GLA_RANK, GLA_RANK,
             ATT_Q, ATT_KV, ATT_KV, LRU_WIDTH, LRU_WIDTH, N_BRANCHES * D_MODEL)

SUBLANES = 8
LANES = 128
TIME_BLOCK = 256

COL_GQ = 0
COL_GK = 512
COL_GV = 1024
COL_GR = 2048
COL_AQ = 3072
COL_LX = 4096
COL_LY = 5120
COL_GATE = 6144
COL_AK = 9216
COL_AV = 9472
COL_LR = 9728
D_IN_PAD = 10240

VMEM_LIMIT = 56 * 1024 * 1024


def _cparams(sem):
    return pltpu.CompilerParams(dimension_semantics=sem, vmem_limit_bytes=VMEM_LIMIT)


def _sigmoid(x):
    return 0.5 * jnp.tanh(0.5 * x) + 0.5


def _silu(x):
    hx = 0.5 * x
    return hx * jnp.tanh(hx) + hx


def _gelu_tanh(x):
    hx = 0.5 * x
    inner = x * (0.7978845608028654 + (0.7978845608028654 * 0.044715) * (x * x))
    return hx * jnp.tanh(inner) + hx


def _split_bf16(x):
    hi = x.astype(BF16)
    lo = (x - hi.astype(F32)).astype(BF16)
    return hi, lo


def _rms(x, g):
    ms = jnp.mean(x * x, axis=-1, keepdims=True)
    return x * lax.rsqrt(ms + NORM_EPS) * g


def _ada_kernel(c_ref, w_ref, b_ref, o_ref):
    c = c_ref[...]
    o_ref[0] = jnp.dot(_silu(c), w_ref[0], preferred_element_type=F32,
                       precision=lax.Precision.HIGHEST) + b_ref[0]


def _ada_call(cvec, w_ada, b_ada):
    depth, d, n = w_ada.shape
    tn = 1536
    rows = cvec.shape[0]
    return pl.pallas_call(
        _ada_kernel,
        grid=(depth, n // tn),
        in_specs=[pl.BlockSpec((rows, d), lambda l, j: (0, 0)),
                  pl.BlockSpec((1, d, tn), lambda l, j: (l, 0, j)),
                  pl.BlockSpec((1, 1, tn), lambda l, j: (l, 0, j))],
        out_specs=pl.BlockSpec((1, rows, tn), lambda l, j: (l, 0, j)),
        out_shape=jax.ShapeDtypeStruct((depth, rows, n), F32),
        compiler_params=_cparams(("parallel", "parallel")),
        name="ada_mod",
    )(cvec, w_ada, b_ada.reshape(depth, 1, n))


def _nmm_kernel(x_ref, g_ref, mod_ref, w_ref, o_ref, h_ref, *, shift_idx, scale_idx, tm, nctx, tiles_per_batch):
    i = pl.program_id(0)

    @pl.when(pl.program_id(1) == 0)
    def _():
        y = _rms(x_ref[...], g_ref[...])
        row = (i % tiles_per_batch) * tm + lax.broadcasted_iota(jnp.int32, (tm, 1), 0)
        is_ctx = row < nctx
        shift = jnp.where(is_ctx, mod_ref[0, shift_idx:shift_idx + 1, :], mod_ref[0, 6 + shift_idx:7 + shift_idx, :])
        scale = jnp.where(is_ctx, mod_ref[0, scale_idx:scale_idx + 1, :], mod_ref[0, 6 + scale_idx:7 + scale_idx, :])
        h_ref[...] = (y * (1.0 + scale) + shift).astype(BF16)

    o_ref[...] = jnp.dot(h_ref[...], w_ref[...], preferred_element_type=F32).astype(o_ref.dtype)


def _nmm_call(x, g, modsel, w, *, shift_idx, scale_idx, tm, tn, t_len, nctx, name):
    m, d = x.shape
    n = w.shape[1]
    tiles_per_batch = t_len // tm
    kern = functools.partial(_nmm_kernel, shift_idx=shift_idx, scale_idx=scale_idx, tm=tm, nctx=nctx,
                             tiles_per_batch=tiles_per_batch)
    return pl.pallas_call(
        kern,
        grid=(m // tm, n // tn),
        in_specs=[pl.BlockSpec((tm, d), lambda i, j: (i, 0)),
                  pl.BlockSpec((1, d), lambda i, j: (0, 0)),
                  pl.BlockSpec((1, 12, d), lambda i, j: (i // tiles_per_batch, 0, 0)),
                  pl.BlockSpec((d, tn), lambda i, j: (0, j))],
        out_specs=pl.BlockSpec((tm, tn), lambda i, j: (i, j)),
        out_shape=jax.ShapeDtypeStruct((m, n), BF16),
        scratch_shapes=[pltpu.VMEM((tm, d), BF16)],
        compiler_params=_cparams(("parallel", "arbitrary")),
        name=name,
    )(x, g, modsel, w)


def _log_sigmoid(z):
    return jnp.minimum(z, 0.0) - jnp.log(1.0 + jnp.exp(-jnp.abs(z)))


def _gla_kernel(qf_ref, kf_ref, vf_ref, lrf_ref, qb_ref, kb_ref, vb_ref, lrb_ref, wd_ref, bd_ref,
                of_ref, ob_ref, stf_ref, stb_ref, *, tb):
    @pl.when(pl.program_id(1) == 0)
    def _():
        stf_ref[...] = jnp.zeros_like(stf_ref)
        stb_ref[...] = jnp.zeros_like(stb_ref)

    fwd = _gla_direction(qf_ref, kf_ref, vf_ref, lrf_ref, wd_ref, bd_ref, of_ref, stf_ref, reverse=False, tb=tb)
    bwd = _gla_direction(qb_ref, kb_ref, vb_ref, lrb_ref, wd_ref, bd_ref, ob_ref, stb_ref, reverse=True, tb=tb)
    done = object()
    running = True
    for _ in range(GLA_PHASE_LAG):
        running = next(fwd, done) is not done
    while running:
        step_f = next(fwd, done)
        step_b = next(bwd, done)
        running = step_f is not done or step_b is not done


def _gla_direction(q_ref, k_ref, v_ref, lr_ref, wd_ref, bd_ref, o_ref, st_ref, *, reverse, tb):
    c = GLA_CHUNK
    nchunk = tb // c
    d = 1 if reverse else 0

    lr = lr_ref[...]
    z = (jnp.dot(lr, wd_ref[d, 0], preferred_element_type=F32)
         + jnp.dot(lr, wd_ref[d, 1], preferred_element_type=F32) + bd_ref[d])
    yield
    log_a = _log_sigmoid(z) * (1.0 / GLA_TEMP)
    r = lax.broadcasted_iota(jnp.int32, (tb, tb), 0)
    s = lax.broadcasted_iota(jnp.int32, (tb, tb), 1)
    keep = ((r // c) == (s // c)) & ((s >= r) if reverse else (s <= r))
    tri = jnp.where(keep, 1.0, 0.0).astype(BF16)
    la_hi, la_lo = _split_bf16(log_a)
    cum = jnp.dot(tri, la_hi, preferred_element_type=F32) + jnp.dot(tri, la_lo, preferred_element_type=F32)
    yield

    last = [(ci * c) if reverse else (ci * c + c - 1) for ci in range(nchunk)]
    tots = [cum[i:i + 1] for i in last]
    tot_rows = jnp.concatenate([jnp.broadcast_to(tt, (c, GLA_QK)) for tt in tots], axis=0)
    e_cum = jnp.exp(cum)
    e_inv = jnp.exp(-cum)
    e_end = jnp.exp(tot_rows - cum)
    e_tot = [jnp.exp(tt) for tt in tots]

    order = range(nchunk - 1, -1, -1) if reverse else range(nchunk)
    for h in range(GLA_HEADS):
        yield
        hs = slice(h * GLA_DK, (h + 1) * GLA_DK)
        vs = slice(h * GLA_DV, (h + 1) * GLA_DV)
        q = q_ref[:, hs].astype(F32) * (GLA_DK ** -0.5)
        k = k_ref[:, hs].astype(F32)
        v = v_ref[:, vs]
        qd = (q * e_cum[:, hs]).astype(BF16)
        ki = (k * e_inv[:, hs]).astype(BF16)
        ke = (k * e_end[:, hs]).astype(BF16)
        att = lax.dot_general(qd, ki, NT_DIMS, preferred_element_type=F32)
        o_intra = jnp.dot(jnp.where(keep, att, 0.0).astype(BF16), v, preferred_element_type=F32)
        st = st_ref[h]
        o_inter = [None] * nchunk
        for ci in order:
            rows = slice(ci * c, (ci + 1) * c)
            o_inter[ci] = lax.dot_general(qd[rows], st.astype(BF16), NT_DIMS, preferred_element_type=F32)
            upd = lax.dot_general(v[rows], ke[rows], (((0,), (0,)), ((), ())), preferred_element_type=F32)
            st = st * e_tot[ci][:, hs] + upd
        st_ref[h] = st
        o_ref[:, vs] = (o_intra + jnp.concatenate(o_inter, axis=0)).astype(o_ref.dtype)


def _time_block_index(t, nblk, reverse):
    if not reverse:
        return t
    return jnp.where(t == 0, 0, nblk - t)


def _gla_call(p, wd_pad, bd, *, batch, t_len):
    m = p.shape[0]
    tb = TIME_BLOCK
    nblk = t_len // tb

    def blocks(reverse):
        def rowblk(b, t):
            return b * nblk + _time_block_index(t, nblk, reverse)
        return [pl.BlockSpec((tb, GLA_QK), lambda b, t: (rowblk(b, t), COL_GQ // GLA_QK)),
                pl.BlockSpec((tb, GLA_QK), lambda b, t: (rowblk(b, t), COL_GK // GLA_QK)),
                pl.BlockSpec((tb, GLA_V), lambda b, t: (rowblk(b, t), COL_GV // GLA_V)),
                pl.BlockSpec((tb, LANES), lambda b, t: (rowblk(b, t), COL_LR // LANES))], \
            pl.BlockSpec((tb, GLA_V), lambda b, t: (rowblk(b, t), 0))

    in_f, out_f = blocks(False)
    in_b, out_b = blocks(True)
    state = pltpu.VMEM((GLA_HEADS, GLA_DV, GLA_DK), F32)
    return pl.pallas_call(
        functools.partial(_gla_kernel, tb=tb),
        grid=(batch, nblk),
        in_specs=in_f + in_b + [pl.BlockSpec((2, 2, LANES, GLA_QK), lambda b, t: (0, 0, 0, 0)),
                                pl.BlockSpec((2, 1, GLA_QK), lambda b, t: (0, 0, 0))],
        out_specs=[out_f, out_b],
        out_shape=[jax.ShapeDtypeStruct((m, GLA_V), BF16)] * 2,
        scratch_shapes=[state, state],
        compiler_params=_cparams(("parallel", "arbitrary")),
        name="gla_scan",
    )(p, p, p, p, p, p, p, p, wd_pad, bd)


def _qkprep_kernel(q_ref, k_ref, v_ref, cos_ref, sin_ref, gq_ref, gk_ref, qo_ref, ko_ref, vo_ref):
    cos = cos_ref[...]
    sin = sin_ref[...]
    lane = lax.broadcasted_iota(jnp.int32, (1, HEAD_DIM), 1)
    first = (lane % (HEAD_DIM // 2)) < (HEAD_DIM // 4)

    def prep(x, g, scale):
        y = _rms(x.astype(F32), g)
        partner = jnp.where(first, pltpu.roll(y, HEAD_DIM - HEAD_DIM // 4, 1), pltpu.roll(y, HEAD_DIM // 4, 1))
        out = y * cos + partner * sin
        if scale != 1.0:
            out = out * scale
        return out.astype(BF16)

    for h in range(ATT_Q_HEADS):
        hs = slice(h * HEAD_DIM, (h + 1) * HEAD_DIM)
        qo_ref[:, hs] = prep(q_ref[:, hs], gq_ref[...], HEAD_DIM ** -0.5 * LOG2_E)
    for h in range(ATT_KV_HEADS):
        hs = slice(h * HEAD_DIM, (h + 1) * HEAD_DIM)
        ko_ref[:, hs] = prep(k_ref[:, hs], gk_ref[...], 1.0)
        vo_ref[:, 2 * h * HEAD_DIM:(2 * h + 1) * HEAD_DIM] = v_ref[:, hs]
        vo_ref[:, (2 * h + 1) * HEAD_DIM:(2 * h + 2) * HEAD_DIM] = jnp.ones((v_ref.shape[0], HEAD_DIM), BF16)


def _qkprep_call(p, cos_t, sin_t, gq, gk, *, t_len):
    m = p.shape[0]
    tb = TIME_BLOCK
    nblk = t_len // tb
    return pl.pallas_call(
        _qkprep_kernel,
        grid=(m // tb,),
        in_specs=[pl.BlockSpec((tb, ATT_Q), lambda i: (i, COL_AQ // ATT_Q)),
                  pl.BlockSpec((tb, ATT_KV), lambda i: (i, COL_AK // ATT_KV)),
                  pl.BlockSpec((tb, ATT_KV), lambda i: (i, COL_AV // ATT_KV)),
                  pl.BlockSpec((tb, HEAD_DIM), lambda i: (i % nblk, 0)),
                  pl.BlockSpec((tb, HEAD_DIM), lambda i: (i % nblk, 0)),
                  pl.BlockSpec((1, HEAD_DIM), lambda i: (0, 0)),
                  pl.BlockSpec((1, HEAD_DIM), lambda i: (0, 0))],
        out_specs=[pl.BlockSpec((tb, ATT_Q), lambda i: (i, 0)),
                   pl.BlockSpec((tb, ATT_KV), lambda i: (i, 0)),
                   pl.BlockSpec((tb, 2 * ATT_KV), lambda i: (i, 0))],
        out_shape=[jax.ShapeDtypeStruct((m, ATT_Q), BF16), jax.ShapeDtypeStruct((m, ATT_KV), BF16),
                   jax.ShapeDtypeStruct((m, 2 * ATT_KV), BF16)],
        compiler_params=_cparams(("parallel",)),
        name="qk_prep",
    )(p, p, p, cos_t, sin_t, gq, gk)


NEG_BIG = -1e30


ATT_ROW_SUB = 64
ATT_HEADS_PER_STEP = 2
NT_DIMS = (((1,), (1,)), ((), ()))


def _attn_kernel(q_ref, k_ref, v_ref, o_ref, s_ref, p_ref, a_ref, m_ref, acc_ref, *, tq, tk, t_len):
    nkv = t_len // tk
    sub = ATT_ROW_SUB
    nlt = tk // LANES
    nh = q_ref.shape[1] // HEAD_DIM
    m_ref[...] = jnp.full_like(m_ref, NEG_BIG)
    acc_ref[...] = jnp.zeros_like(acc_ref)
    items = [(h, j) for h in range(nh) for j in range(nkv)]

    def scores(i):
        h, j = items[i]
        q = q_ref[:, h * HEAD_DIM:(h + 1) * HEAD_DIM]
        s_ref[i % 2] = lax.dot_general(q, k_ref[j * tk:(j + 1) * tk, :], NT_DIMS, preferred_element_type=F32)

    def exponentials(i):
        h, _ = items[i]
        slot = i % 2
        for r in range(tq // sub):
            rows = slice(r * sub, (r + 1) * sub)
            tiles = [s_ref[slot, rows, t * LANES:(t + 1) * LANES] for t in range(nlt)]
            mx = tiles[0]
            for t in range(1, nlt):
                mx = jnp.maximum(mx, tiles[t])
            m_prev = m_ref[h, rows, :]
            m_new = jnp.maximum(m_prev, jnp.max(mx, axis=1, keepdims=True))
            a_ref[slot, rows, :] = jnp.exp2(m_prev - m_new)
            m_ref[h, rows, :] = m_new
            for t in range(nlt):
                p_ref[slot, rows, t * LANES:(t + 1) * LANES] = jnp.exp2(tiles[t] - m_new).astype(BF16)

    def weighted_values(i):
        h, j = items[i]
        slot = i % 2
        pv = jnp.dot(p_ref[slot], v_ref[j * tk:(j + 1) * tk, :], preferred_element_type=F32)
        alpha = a_ref[slot]
        acc_ref[h] = jnp.concatenate([alpha, alpha], axis=1) * acc_ref[h] + pv

    n = len(items)
    for step in range(n + 2):
        if step < n:
            scores(step)
        if 1 <= step <= n:
            exponentials(step - 1)
        if step >= 2:
            weighted_values(step - 2)

    for h in range(nh):
        acc = acc_ref[h]
        o_ref[:, h * HEAD_DIM:(h + 1) * HEAD_DIM] = (acc[:, :HEAD_DIM] / acc[:, HEAD_DIM:]).astype(o_ref.dtype)


def _attn_call(qn, kn, vext, *, batch, t_len, tq):
    m = qn.shape[0]
    nq = t_len // tq
    tk = tq
    assert tq % ATT_ROW_SUB == 0 and tk % LANES == 0
    kern = functools.partial(_attn_kernel, tq=tq, tk=tk, t_len=t_len)
    nh = ATT_HEADS_PER_STEP
    hsteps = ATT_GROUP // nh
    width = nh * HEAD_DIM
    return pl.pallas_call(
        kern,
        grid=(batch, ATT_KV_HEADS, nq, hsteps),
        in_specs=[pl.BlockSpec((tq, width), lambda b, g, i, h: (b * nq + i, g * hsteps + h)),
                  pl.BlockSpec((t_len, HEAD_DIM), lambda b, g, i, h: (b, g)),
                  pl.BlockSpec((t_len, 2 * HEAD_DIM), lambda b, g, i, h: (b, g))],
        out_specs=pl.BlockSpec((tq, width), lambda b, g, i, h: (b * nq + i, g * hsteps + h)),
        out_shape=jax.ShapeDtypeStruct((m, ATT_Q), BF16),
        scratch_shapes=[pltpu.VMEM((2, tq, tk), F32), pltpu.VMEM((2, tq, tk), BF16),
                        pltpu.VMEM((2, tq, LANES), F32), pltpu.VMEM((nh, tq, LANES), F32),
                        pltpu.VMEM((nh, tq, 2 * HEAD_DIM), F32)],
        compiler_params=_cparams(("parallel", "parallel", "arbitrary", "arbitrary")),
        name="flash_attn",
    )(qn, kn, vext)


def _attn_ctx_kernel(q_ref, k_ref, v_ref, att_ref, o_ref):
    del att_ref
    k = k_ref[...]
    v = v_ref[...]
    for h in range(ATT_GROUP):
        hs = slice(h * HEAD_DIM, (h + 1) * HEAD_DIM)
        s = lax.dot_general(q_ref[:, hs], k, NT_DIMS, preferred_element_type=F32)
        pexp = jnp.exp2(s - jnp.max(s, axis=1, keepdims=True)).astype(BF16)
        acc = jnp.dot(pexp, v, preferred_element_type=F32)
        o_ref[:, hs] = (acc[:, :HEAD_DIM] / acc[:, HEAD_DIM:]).astype(o_ref.dtype)


def _attn_ctx_call(qn, kn, vext, att, *, batch, t_len, nctx):
    nblk = t_len // nctx
    width = ATT_GROUP * HEAD_DIM
    return pl.pallas_call(
        _attn_ctx_kernel,
        grid=(batch, ATT_KV_HEADS),
        in_specs=[pl.BlockSpec((nctx, width), lambda b, g: (b * nblk, g)),
                  pl.BlockSpec((nctx, HEAD_DIM), lambda b, g: (b * nblk, g)),
                  pl.BlockSpec((nctx, 2 * HEAD_DIM), lambda b, g: (b * nblk, g)),
                  pl.BlockSpec(memory_space=pl.ANY)],
        out_specs=pl.BlockSpec((nctx, width), lambda b, g: (b * nblk, g)),
        out_shape=jax.ShapeDtypeStruct(att.shape, att.dtype),
        input_output_aliases={3: 0},
        compiler_params=_cparams(("parallel", "parallel")),
        name="ctx_attn",
    )(qn, kn, vext, att)


def _halo_valid(blk, nblk):
    prev_ok = blk >= 2
    next_ok = jnp.logical_and(blk != 0, blk != nblk - 1)
    return prev_ok, next_ok


def _shift_matrices(tb, offsets):
    r = jnp.arange(tb)[:, None]
    c = jnp.arange(tb)[None, :]
    return jnp.stack([(c == r + off) for off in offsets]).astype(BF16)


def _lru_kernel(x_ref, xp_ref, xn_ref, sh_ref, cw_ref, cb_ref, wg_ref, bg_ref, lam_ref, h_ref,
                carry_ref, a_ref, u_ref, hs_ref, hl_ref, ac_ref, *, reverse, tb, nblk):
    t = pl.program_id(1)
    blk = _time_block_index(t, nblk, reverse)

    @pl.when(t == 0)
    def _():
        carry_ref[...] = jnp.zeros_like(carry_ref)

    prev_ok, next_ok = _halo_valid(blk, nblk)
    xb = x_ref[...]
    x = xb.astype(F32)
    prev = jnp.where(prev_ok, xp_ref[...].astype(F32), 0.0)
    nxt = jnp.where(next_ok, xn_ref[...].astype(F32), 0.0)
    xm2 = jnp.dot(sh_ref[0], xb, preferred_element_type=F32)
    xm1 = jnp.dot(sh_ref[1], xb, preferred_element_type=F32)
    xp1 = jnp.dot(sh_ref[2], xb, preferred_element_type=F32)
    xc = cw_ref[0:1] * xm2 + cw_ref[1:2] * xm1 + cw_ref[2:3] * x + cw_ref[3:4] * xp1 + cb_ref[...]
    r8 = lax.broadcasted_iota(jnp.int32, (SUBLANES, 1), 0)
    p6 = prev[SUBLANES - 2:SUBLANES - 1]
    p7 = prev[SUBLANES - 1:SUBLANES]
    fix_first = jnp.where(r8 == 0, cw_ref[0:1] * p6 + cw_ref[1:2] * p7, jnp.where(r8 == 1, cw_ref[0:1] * p7, 0.0))
    fix_last = jnp.where(r8 == SUBLANES - 1, cw_ref[3:4] * nxt[0:1], 0.0)
    xc = jnp.concatenate([xc[:SUBLANES] + fix_first, xc[SUBLANES:tb - SUBLANES], xc[tb - SUBLANES:] + fix_last],
                         axis=0)

    xcb = xc.astype(BF16)
    zs = []
    for gi in range(2):
        parts = [jnp.dot(xcb[:, hb * LRU_BW:(hb + 1) * LRU_BW], wg_ref[0, gi, hb], preferred_element_type=F32)
                 for hb in range(LRU_BLOCKS)]
        zs.append(jnp.concatenate(parts, axis=-1) + bg_ref[0, gi:gi + 1, :])
    gate_r = _sigmoid(zs[0])
    gate_i = _sigmoid(zs[1])
    neg_lam = -lam_ref[0]
    softplus = jnp.maximum(neg_lam, 0.0) + jnp.log(1.0 + jnp.exp(-jnp.abs(neg_lam)))
    log_a = (-LRU_C) * gate_r * softplus
    a = jnp.exp(log_a)
    one_m_a2 = jnp.tanh(-log_a) * (1.0 + a * a)
    root = jnp.where(one_m_a2 > 0.0, one_m_a2 * lax.rsqrt(one_m_a2), 0.0)
    u = root * (gate_i * xc)

    ncol = LRU_WIDTH // LANES
    for cb in range(ncol):
        a_ref[cb] = a[:, cb * LANES:(cb + 1) * LANES]
        u_ref[cb] = u[:, cb * LANES:(cb + 1) * LANES]
    seg = LRU_SEG
    grp_rows = SUBLANES * seg
    ngrp = tb // grp_rows

    def tile_rows(grp, j):
        return pl.ds(grp * grp_rows + j, SUBLANES, stride=seg)

    def strided(ref, grp, j):
        return jnp.concatenate([ref[cb, tile_rows(grp, j), :] for cb in range(ncol)], axis=1)

    steps = range(seg - 1, -1, -1) if reverse else range(seg)
    summaries = []
    for grp in range(ngrp):
        h_loc = jnp.zeros((SUBLANES, LRU_WIDTH), F32)
        a_cum = jnp.ones((SUBLANES, LRU_WIDTH), F32)
        for j in steps:
            a_j = strided(a_ref, grp, j)
            u_j = strided(u_ref, grp, j)
            h_loc = a_j * h_loc + u_j
            a_cum = a_j * a_cum
            rows = slice((grp * seg + j) * SUBLANES, (grp * seg + j + 1) * SUBLANES)
            hl_ref[rows, :] = h_loc
            ac_ref[rows, :] = a_cum
        e, pr = h_loc, a_cum
        for sft in (1, 2, 4):
            if reverse:
                e_sh, p_sh = pltpu.roll(e, SUBLANES - sft, 0), pltpu.roll(pr, SUBLANES - sft, 0)
                valid = r8 < SUBLANES - sft
            else:
                e_sh, p_sh, valid = pltpu.roll(e, sft, 0), pltpu.roll(pr, sft, 0), r8 >= sft
            e = jnp.where(valid, e + pr * e_sh, e)
            pr = jnp.where(valid, pr * p_sh, pr)
        summaries.append((e, pr))

    c0 = carry_ref[...]
    c_in = [None] * ngrp
    for grp in (range(ngrp - 1, -1, -1) if reverse else range(ngrp)):
        e, pr = summaries[grp]
        g = e + pr * c0
        if reverse:
            c_in[grp] = jnp.where(r8 == SUBLANES - 1, c0, pltpu.roll(g, SUBLANES - 1, 0))
            c0 = jnp.broadcast_to(g[0:1], (SUBLANES, LRU_WIDTH))
        else:
            c_in[grp] = jnp.where(r8 == 0, c0, pltpu.roll(g, 1, 0))
            c0 = jnp.broadcast_to(g[SUBLANES - 1:SUBLANES], (SUBLANES, LRU_WIDTH))
    carry_ref[...] = c0

    for grp in range(ngrp):
        for j in range(seg):
            rows = slice((grp * seg + j) * SUBLANES, (grp * seg + j + 1) * SUBLANES)
            h_j = hl_ref[rows, :] + ac_ref[rows, :] * c_in[grp]
            for cb in range(ncol):
                hs_ref[cb, tile_rows(grp, j), :] = h_j[:, cb * LANES:(cb + 1) * LANES]
    for cb in range(ncol):
        h_ref[:, cb * LANES:(cb + 1) * LANES] = hs_ref[cb].astype(h_ref.dtype)


def _lru_call(p, cw, cb, wg, bg, lam, *, reverse, batch, t_len):
    m = p.shape[0]
    tb = TIME_BLOCK
    nblk = t_len // tb
    per = tb // SUBLANES
    nrow8 = m // SUBLANES
    d = 1 if reverse else 0

    def rowblk(b, t):
        return b * nblk + _time_block_index(t, nblk, reverse)

    kern = functools.partial(_lru_kernel, reverse=reverse, tb=tb, nblk=nblk)
    return pl.pallas_call(
        kern,
        grid=(batch, nblk),
        in_specs=[pl.BlockSpec((tb, LRU_WIDTH), lambda b, t: (rowblk(b, t), COL_LX // LRU_WIDTH)),
                  pl.BlockSpec((SUBLANES, LRU_WIDTH),
                               lambda b, t: (jnp.maximum(rowblk(b, t) * per - 1, 0), COL_LX // LRU_WIDTH)),
                  pl.BlockSpec((SUBLANES, LRU_WIDTH),
                               lambda b, t: (jnp.minimum((rowblk(b, t) + 1) * per, nrow8 - 1), COL_LX // LRU_WIDTH)),
                  pl.BlockSpec((LRU_CONV - 1, tb, tb), lambda b, t: (0, 0, 0)),
                  pl.BlockSpec((LRU_CONV, LRU_WIDTH), lambda b, t: (0, 0)),
                  pl.BlockSpec((1, LRU_WIDTH), lambda b, t: (0, 0)),
                  pl.BlockSpec((1, 2, LRU_BLOCKS, LRU_BW, LRU_BW), lambda b, t: (d, 0, 0, 0, 0)),
                  pl.BlockSpec((1, 2, LRU_WIDTH), lambda b, t: (d, 0, 0)),
                  pl.BlockSpec((1, 1, LRU_WIDTH), lambda b, t: (d, 0, 0))],
        out_specs=pl.BlockSpec((tb, LRU_WIDTH), lambda b, t: (rowblk(b, t), 0)),
        out_shape=jax.ShapeDtypeStruct((m, LRU_WIDTH), BF16),
        scratch_shapes=([pltpu.VMEM((SUBLANES, LRU_WIDTH), F32)]
                        + [pltpu.VMEM((LRU_WIDTH // LANES, tb, LANES), F32)] * 3
                        + [pltpu.VMEM((tb, LRU_WIDTH), F32)] * 2),
        compiler_params=_cparams(("parallel", "arbitrary")),
        name="lru_bwd" if reverse else "lru_fwd",
    )(p, p, p, _shift_matrices(tb, (-2, -1, 1)), cw, cb, wg, bg, lam)


def _select_mod(mod_ref, idx, i, tm, tiles_per_batch, nctx):
    row = (i % tiles_per_batch) * tm + lax.broadcasted_iota(jnp.int32, (tm, 1), 0)
    return jnp.where(row < nctx, mod_ref[0, idx:idx + 1, :], mod_ref[0, 6 + idx:7 + idx, :])


def _merge_kernel(of_ref, ob_ref, r_ref, att_ref, hf_ref, hb_ref, y_ref, gate_ref, x_ref, mod_ref,
                  gn_ref, gp_ref, wb_ref, wo_ref, o_ref, *, tm, tiles_per_batch, nctx):
    i = pl.program_id(0)
    o = of_ref[...].astype(F32) + ob_ref[...].astype(F32)
    gla_parts = []
    for h in range(GLA_HEADS):
        vs = slice(h * GLA_DV, (h + 1) * GLA_DV)
        gla_parts.append(_rms(o[:, vs], gn_ref[:, vs]))
    gla = jnp.concatenate(gla_parts, axis=-1) * _silu(r_ref[...].astype(F32))
    lru = (hf_ref[...].astype(F32) + hb_ref[...].astype(F32)) * _gelu_tanh(y_ref[...].astype(F32))
    branches = (gla.astype(BF16), att_ref[...], lru.astype(BF16))
    mixed = None
    for bi in range(N_BRANCHES):
        gate = _sigmoid(gate_ref[:, bi * D_MODEL:(bi + 1) * D_MODEL].astype(F32))
        term = gate * jnp.dot(branches[bi], wb_ref[bi], preferred_element_type=F32)
        mixed = term if mixed is None else mixed + term
    y = jnp.dot(mixed.astype(BF16), wo_ref[...], preferred_element_type=F32)
    gate_vec = _select_mod(mod_ref, 2, i, tm, tiles_per_batch, nctx)
    o_ref[...] = x_ref[...] + gate_vec * _rms(y, gp_ref[...])


def _merge_call(o_f, o_b, p, att, h_f, h_b, x, modsel, gn, gp, wb, wo, *, tm, t_len, nctx):
    m, d = x.shape
    tiles_per_batch = t_len // tm
    kern = functools.partial(_merge_kernel, tm=tm, tiles_per_batch=tiles_per_batch, nctx=nctx)
    row = lambda i: (i, 0)
    const2 = lambda i: (0, 0)
    return pl.pallas_call(
        kern,
        grid=(m // tm,),
        in_specs=[pl.BlockSpec((tm, d), row),
                  pl.BlockSpec((tm, d), row),
                  pl.BlockSpec((tm, d), lambda i: (i, COL_GR // D_MODEL)),
                  pl.BlockSpec((tm, d), row),
                  pl.BlockSpec((tm, d), row),
                  pl.BlockSpec((tm, d), row),
                  pl.BlockSpec((tm, d), lambda i: (i, COL_LY // D_MODEL)),
                  pl.BlockSpec((tm, N_BRANCHES * d), lambda i: (i, COL_GATE // (N_BRANCHES * D_MODEL))),
                  pl.BlockSpec((tm, d), row),
                  pl.BlockSpec((1, 12, d), lambda i: (i // tiles_per_batch, 0, 0)),
                  pl.BlockSpec((1, d), const2),
                  pl.BlockSpec((1, d), const2),
                  pl.BlockSpec((N_BRANCHES, d, d), lambda i: (0, 0, 0)),
                  pl.BlockSpec((d, d), const2)],
        out_specs=pl.BlockSpec((tm, d), row),
        out_shape=jax.ShapeDtypeStruct((m, d), F32),
        compiler_params=_cparams(("parallel",)),
        name="merge_out",
    )(o_f, o_b, p, att, h_f, h_b, p, p, x, modsel, gn, gp, wb, wo)


def _ffn_down_kernel(u_ref, up_ref, un_ref, cw_ref, cb_ref, wd_ref, x_ref, mod_ref, gp_ref, o_ref,
                     *, tb, nblk):
    i = pl.program_id(0)
    blk = i % nblk
    prev_ok, next_ok = _halo_valid(blk, nblk)
    r8 = lax.broadcasted_iota(jnp.int32, (SUBLANES, 1), 0)

    def conv(cols):
        u = u_ref[:, cols].astype(F32)
        prev = jnp.where(prev_ok, up_ref[SUBLANES - 1:SUBLANES, cols].astype(F32), 0.0)
        nxt = jnp.where(next_ok, un_ref[0:1, cols].astype(F32), 0.0)
        um1 = pltpu.roll(u, 1, 0)
        up1 = pltpu.roll(u, tb - 1, 0)
        um1 = jnp.concatenate([jnp.where(r8 == 0, prev, um1[:SUBLANES]), um1[SUBLANES:]], axis=0)
        up1 = jnp.concatenate([up1[:tb - SUBLANES], jnp.where(r8 == SUBLANES - 1, nxt, up1[tb - SUBLANES:])], axis=0)
        return cw_ref[0:1, cols] * um1 + cw_ref[1:2, cols] * u + cw_ref[2:3, cols] * up1 + cb_ref[:, cols]

    bounds = list(range(0, D_FF, FFN_COL_CHUNK)) + [D_FF]
    y = None
    for lo, hi in zip(bounds[:-1], bounds[1:]):
        act = (_silu(conv(slice(lo, hi))) * conv(slice(D_FF + lo, D_FF + hi))).astype(BF16)
        part = jnp.dot(act, wd_ref[lo:hi, :], preferred_element_type=F32)
        y = part if y is None else y + part
    gate_vec = _select_mod(mod_ref, 5, i, tb, nblk, tb)
    o_ref[...] = x_ref[...] + gate_vec * _rms(y, gp_ref[...])


def _ffn_down_call(u, cw, cb, wd, x, modsel, gp, *, t_len, latent_only):
    m, d = x.shape
    tb = TIME_BLOCK
    nblk = t_len // tb
    per = tb // SUBLANES
    nrow8 = m // SUBLANES
    kern = functools.partial(_ffn_down_kernel, tb=tb, nblk=nblk)
    if latent_only:
        out_rows = m // nblk * (nblk - 1)
        out_map = lambda i: ((i // nblk) * (nblk - 1) + jnp.maximum(i % nblk - 1, 0), 0)
    else:
        out_rows = m
        out_map = lambda i: (i, 0)
    return pl.pallas_call(
        kern,
        grid=(m // tb,),
        in_specs=[pl.BlockSpec((tb, 2 * D_FF), lambda i: (i, 0)),
                  pl.BlockSpec((SUBLANES, 2 * D_FF), lambda i: (jnp.maximum(i * per - 1, 0), 0)),
                  pl.BlockSpec((SUBLANES, 2 * D_FF), lambda i: (jnp.minimum((i + 1) * per, nrow8 - 1), 0)),
                  pl.BlockSpec((FFN_CONV, 2 * D_FF), lambda i: (0, 0)),
                  pl.BlockSpec((1, 2 * D_FF), lambda i: (0, 0)),
                  pl.BlockSpec((D_FF, d), lambda i: (0, 0)),
                  pl.BlockSpec((tb, d), lambda i: (i, 0)),
                  pl.BlockSpec((1, 12, d), lambda i: (i // nblk, 0, 0)),
                  pl.BlockSpec((1, d), lambda i: (0, 0))],
        out_specs=pl.BlockSpec((tb, d), out_map),
        out_shape=jax.ShapeDtypeStruct((out_rows, d), F32),
        compiler_params=_cparams(("arbitrary",)),
        name="ffn_down",
    )(u, u, u, cw, cb, wd, x, modsel, gp)


def _rope_tables(seq, nctx):
    rows = seq // GRID_W
    row = np.repeat(np.arange(rows), GRID_W)
    col = np.tile(np.arange(GRID_W), rows)
    axis_dim = HEAD_DIM // 2
    inv_freq = (1.0 / (np.float32(ROPE_THETA) ** (np.arange(0, axis_dim, 2, dtype=np.float32) / np.float32(axis_dim))))
    ang = np.stack([row, col], axis=-1).astype(np.float32)[:, :, None] * inv_freq.astype(np.float32)
    cos, sin = np.cos(ang), np.sin(ang)
    cos_t = np.concatenate([cos[:, 0], cos[:, 0], cos[:, 1], cos[:, 1]], axis=-1)
    sin_t = np.concatenate([-sin[:, 0], sin[:, 0], -sin[:, 1], sin[:, 1]], axis=-1)
    cos_t = np.concatenate([np.ones((nctx, HEAD_DIM), np.float32), cos_t], axis=0)
    sin_t = np.concatenate([np.zeros((nctx, HEAD_DIM), np.float32), sin_t], axis=0)
    return jnp.asarray(cos_t, F32), jnp.asarray(sin_t, F32)


def _permute_w_in(w):
    offs = [0]
    for s in IN_SPLITS:
        offs.append(offs[-1] + s)
    seg = lambda i: w[:, offs[i]:offs[i + 1]]
    gq, gk, gv, gr, lrf, lrb, aq, ak, av, lx, ly, gates = (seg(i) for i in range(12))
    pad = jnp.zeros((w.shape[0], D_IN_PAD - COL_LR - 2 * GLA_RANK), w.dtype)
    return jnp.concatenate([gq, gk, gv, gr, aq, lx, ly, gates, ak, av, lrf, lrb, pad], axis=1).astype(BF16)


def _largest_tile(t_len, cap):
    best = SUBLANES
    for cand in range(SUBLANES, cap + 1, SUBLANES):
        if t_len % cand == 0:
            best = cand
    return best


def kernel(x, c, ctx, c_ctx, w_ada, b_ada, norm_gains, w_in, gla_w_decay, gla_b_decay, gla_norm_g,
           att_q_norm_g, att_k_norm_g, lru_conv_w, lru_conv_b, lru_w_gates, lru_b_gates, lru_lambda,
           w_branch, w_out, ffn_w_up, ffn_conv_w, ffn_conv_b, ffn_w_down):
    batch, seq, d = x.shape
    nctx = ctx.shape[1]
    depth = w_ada.shape[0]
    assert d == D_MODEL and nctx == TIME_BLOCK and seq % TIME_BLOCK == 0 and seq % GRID_W == 0
    t_len = nctx + seq
    m = batch * t_len
    tm_big = _largest_tile(t_len, 1408)
    tq = _largest_tile(t_len, 768)
    assert tm_big % TIME_BLOCK == 0 or tm_big % LANES == 0
    assert tq % TIME_BLOCK == 0

    xc = jnp.concatenate([ctx, x], axis=1).reshape(m, d)

    rows = -(-(batch + 1) // SUBLANES) * SUBLANES
    cvec = jnp.zeros((rows, d), F32).at[:batch].set(c).at[batch].set(c_ctx)
    mod = _ada_call(cvec, w_ada, b_ada)
    mod = mod.reshape(depth, rows, 6, d)
    cos_t, sin_t = _rope_tables(seq, nctx)

    for l in range(depth):
        mod_c = jnp.broadcast_to(mod[l, batch][None], (batch, 6, d))
        modsel = jnp.concatenate([mod_c, mod[l, :batch]], axis=1)
        g_pre_mix, g_post_mix, g_pre_ffn, g_post_ffn = (norm_gains[l, i].reshape(1, d) for i in range(4))

        p = _nmm_call(xc, g_pre_mix, modsel, _permute_w_in(w_in[l]), shift_idx=0, scale_idx=1,
                      tm=tm_big, tn=1024, t_len=t_len, nctx=nctx, name="in_proj")

        wd_pad = jnp.zeros((2, LANES, GLA_QK), F32)
        wd_pad = wd_pad.at[0, :GLA_RANK].set(gla_w_decay[l, 0]).at[1, GLA_RANK:2 * GLA_RANK].set(gla_w_decay[l, 1])
        wd_pad = jnp.stack(_split_bf16(wd_pad), axis=1)
        bd = gla_b_decay[l].reshape(2, 1, GLA_QK)
        o_f, o_b = _gla_call(p, wd_pad, bd, batch=batch, t_len=t_len)

        qn, kn, vext = _qkprep_call(p, cos_t, sin_t, att_q_norm_g[l].reshape(1, HEAD_DIM),
                                    att_k_norm_g[l].reshape(1, HEAD_DIM), t_len=t_len)
        att = _attn_call(qn, kn, vext, batch=batch, t_len=t_len, tq=tq)
        if l < depth - 1:
            att = _attn_ctx_call(qn, kn, vext, att, batch=batch, t_len=t_len, nctx=nctx)

        wg = lru_w_gates[l].astype(BF16)
        lam = lru_lambda[l].reshape(2, 1, LRU_WIDTH)
        cb = lru_conv_b[l].reshape(1, LRU_WIDTH)
        h_f = _lru_call(p, lru_conv_w[l], cb, wg, lru_b_gates[l], lam, reverse=False, batch=batch, t_len=t_len)
        h_b = _lru_call(p, lru_conv_w[l], cb, wg, lru_b_gates[l], lam, reverse=True, batch=batch, t_len=t_len)

        xc = _merge_call(o_f, o_b, p, att, h_f, h_b, xc, modsel, gla_norm_g[l].reshape(1, GLA_V), g_post_mix,
                         w_branch[l].astype(BF16), w_out[l].astype(BF16), tm=_largest_tile(t_len, 384),
                         t_len=t_len, nctx=nctx)

        u = _nmm_call(xc, g_pre_ffn, modsel, ffn_w_up[l].astype(BF16), shift_idx=3, scale_idx=4,
                      tm=_largest_tile(t_len, 704), tn=D_FF, t_len=t_len, nctx=nctx, name="ffn_up")
        xc = _ffn_down_call(u, ffn_conv_w[l], ffn_conv_b[l].reshape(1, 2 * D_FF), ffn_w_down[l].astype(BF16),
                            xc, modsel, g_post_ffn, t_len=t_len, latent_only=(l == depth - 1))

    return xc.reshape(batch, seq, d)
```

```python
import functools

import jax
import jax.numpy as jnp
import numpy as np
from jax import lax
from jax.experimental import pallas as pl
from jax.experimental.pallas import tpu as pltpu

F32 = jnp.float32
BF16 = jnp.bfloat16

D_MODEL = 1024
NORM_EPS = 1e-6
N_BRANCHES = 3
GRID_W = 64

GLA_HEADS = 4
GLA_DK = D_MODEL // (2 * GLA_HEADS)
GLA_DV = D_MODEL // GLA_HEADS
GLA_QK = GLA_HEADS * GLA_DK
GLA_V = GLA_HEADS * GLA_DV
GLA_RANK = 16
GLA_TEMP = 16.0
GLA_CHUNK = 64
GLA_PHASE_LAG = 3

HEAD_DIM = 128
ATT_Q_HEADS = D_MODEL // HEAD_DIM
ATT_KV_HEADS = ATT_Q_HEADS // 4
ATT_GROUP = ATT_Q_HEADS // ATT_KV_HEADS
ATT_Q = ATT_Q_HEADS * HEAD_DIM
ATT_KV = ATT_KV_HEADS * HEAD_DIM
ROPE_THETA = 10000.0
LOG2_E = 1.4426950408889634

LRU_WIDTH = D_MODEL
LRU_BLOCKS = 8
LRU_BW = LRU_WIDTH // LRU_BLOCKS
LRU_C = 8.0
LRU_CONV = 4
LRU_CONV_LEFT = 2
LRU_SEG = 4

D_FF = 2816
FFN_CONV = 3
FFN_COL_CHUNK = 256

IN_SPLITS = (GLA_QK, GLA_QK, GLA_V, GLA_V, GLA_RANK, GLA_RANK,
             ATT_Q, ATT_KV, ATT_KV, LRU_WIDTH, LRU_WIDTH, N_BRANCHES * D_MODEL)

SUBLANES = 8
LANES = 128
TIME_BLOCK = 256

COL_GQ = 0
COL_GK = 512
COL_GV = 1024
COL_GR = 2048
COL_AQ = 3072
COL_LX = 4096
COL_LY = 5120
COL_GATE = 6144
COL_AK = 9216
COL_AV = 9472
COL_LR = 9728
D_IN_PAD = 10240

VMEM_LIMIT = 56 * 1024 * 1024


def _cparams(sem):
    return pltpu.CompilerParams(dimension_semantics=sem, vmem_limit_bytes=VMEM_LIMIT)


def _sigmoid(x):
    return 0.5 * jnp.tanh(0.5 * x) + 0.5


def _silu(x):
    hx = 0.5 * x
    return hx * jnp.tanh(hx) + hx


def _gelu_tanh(x):
    hx = 0.5 * x
    inner = x * (0.7978845608028654 + (0.7978845608028654 * 0.044715) * (x * x))
    return hx * jnp.tanh(inner) + hx


def _split_bf16(x):
    hi = x.astype(BF16)
    lo = (x - hi.astype(F32)).astype(BF16)
    return hi, lo


def _rms(x, g):
    ms = jnp.mean(x * x, axis=-1, keepdims=True)
    return x * lax.rsqrt(ms + NORM_EPS) * g


def _ada_kernel(c_ref, w_ref, b_ref, o_ref):
    c = c_ref[...]
    o_ref[0] = jnp.dot(_silu(c), w_ref[0], preferred_element_type=F32,
                       precision=lax.Precision.HIGHEST) + b_ref[0]


def _ada_call(cvec, w_ada, b_ada):
    depth, d, n = w_ada.shape
    tn = 1536
    rows = cvec.shape[0]
    return pl.pallas_call(
        _ada_kernel,
        grid=(depth, n // tn),
        in_specs=[pl.BlockSpec((rows, d), lambda l, j: (0, 0)),
                  pl.BlockSpec((1, d, tn), lambda l, j: (l, 0, j)),
                  pl.BlockSpec((1, 1, tn), lambda l, j: (l, 0, j))],
        out_specs=pl.BlockSpec((1, rows, tn), lambda l, j: (l, 0, j)),
        out_shape=jax.ShapeDtypeStruct((depth, rows, n), F32),
        compiler_params=_cparams(("parallel", "parallel")),
        name="ada_mod",
    )(cvec, w_ada, b_ada.reshape(depth, 1, n))


def _nmm_kernel(x_ref, g_ref, mod_ref, w_ref, o_ref, h_ref, *, shift_idx, scale_idx, tm, nctx, tiles_per_batch):
    i = pl.program_id(0)

    @pl.when(pl.program_id(1) == 0)
    def _():
        y = _rms(x_ref[...], g_ref[...])
        row = (i % tiles_per_batch) * tm + lax.broadcasted_iota(jnp.int32, (tm, 1), 0)
        is_ctx = row < nctx
        shift = jnp.where(is_ctx, mod_ref[0, shift_idx:shift_idx + 1, :], mod_ref[0, 6 + shift_idx:7 + shift_idx, :])
        scale = jnp.where(is_ctx, mod_ref[0, scale_idx:scale_idx + 1, :], mod_ref[0, 6 + scale_idx:7 + scale_idx, :])
        h_ref[...] = (y * (1.0 + scale) + shift).astype(BF16)

    o_ref[...] = jnp.dot(h_ref[...], w_ref[...], preferred_element_type=F32).astype(o_ref.dtype)


def _nmm_call(x, g, modsel, w, *, shift_idx, scale_idx, tm, tn, t_len, nctx, name):
    m, d = x.shape
    n = w.shape[1]
    tiles_per_batch = t_len // tm
    kern = functools.partial(_nmm_kernel, shift_idx=shift_idx, scale_idx=scale_idx, tm=tm, nctx=nctx,
                             tiles_per_batch=tiles_per_batch)
    return pl.pallas_call(
        kern,
        grid=(m // tm, n // tn),
        in_specs=[pl.BlockSpec((tm, d), lambda i, j: (i, 0)),
                  pl.BlockSpec((1, d), lambda i, j: (0, 0)),
                  pl.BlockSpec((1, 12, d), lambda i, j: (i // tiles_per_batch, 0, 0)),
                  pl.BlockSpec((d, tn), lambda i, j: (0, j))],
        out_specs=pl.BlockSpec((tm, tn), lambda i, j: (i, j)),
        out_shape=jax.ShapeDtypeStruct((m, n), BF16),
        scratch_shapes=[pltpu.VMEM((tm, d), BF16)],
        compiler_params=_cparams(("parallel", "arbitrary")),
        name=name,
    )(x, g, modsel, w)


def _log_sigmoid(z):
    return jnp.minimum(z, 0.0) - jnp.log(1.0 + jnp.exp(-jnp.abs(z)))


def _gla_kernel(qf_ref, kf_ref, vf_ref, lrf_ref, qb_ref, kb_ref, vb_ref, lrb_ref, wd_ref, bd_ref,
                of_ref, ob_ref, stf_ref, stb_ref, *, tb):
    @pl.when(pl.program_id(1) == 0)
    def _():
        stf_ref[...] = jnp.zeros_like(stf_ref)
        stb_ref[...] = jnp.zeros_like(stb_ref)

    fwd = _gla_direction(qf_ref, kf_ref, vf_ref, lrf_ref, wd_ref, bd_ref, of_ref, stf_ref, reverse=False, tb=tb)
    bwd = _gla_direction(qb_ref, kb_ref, vb_ref, lrb_ref, wd_ref, bd_ref, ob_ref, stb_ref, reverse=True, tb=tb)
    done = object()
    running = True
    for _ in range(GLA_PHASE_LAG):
        running = next(fwd, done) is not done
    while running:
        step_f = next(fwd, done)
        step_b = next(bwd, done)
        running = step_f is not done or step_b is not done


def _gla_direction(q_ref, k_ref, v_ref, lr_ref, wd_ref, bd_ref, o_ref, st_ref, *, reverse, tb):
    c = GLA_CHUNK
    nchunk = tb // c
    d = 1 if reverse else 0

    lr = lr_ref[...]
    z = (jnp.dot(lr, wd_ref[d, 0], preferred_element_type=F32)
         + jnp.dot(lr, wd_ref[d, 1], preferred_element_type=F32) + bd_ref[d])
    yield
    log_a = _log_sigmoid(z) * (1.0 / GLA_TEMP)
    r = lax.broadcasted_iota(jnp.int32, (tb, tb), 0)
    s = lax.broadcasted_iota(jnp.int32, (tb, tb), 1)
    keep = ((r // c) == (s // c)) & ((s >= r) if reverse else (s <= r))
    tri = jnp.where(keep, 1.0, 0.0).astype(BF16)
    la_hi, la_lo = _split_bf16(log_a)
    cum = jnp.dot(tri, la_hi, preferred_element_type=F32) + jnp.dot(tri, la_lo, preferred_element_type=F32)
    yield

    last = [(ci * c) if reverse else (ci * c + c - 1) for ci in range(nchunk)]
    tots = [cum[i:i + 1] for i in last]
    tot_rows = jnp.concatenate([jnp.broadcast_to(tt, (c, GLA_QK)) for tt in tots], axis=0)
    e_cum = jnp.exp(cum)
    e_inv = jnp.exp(-cum)
    e_end = jnp.exp(tot_rows - cum)
    e_tot = [jnp.exp(tt) for tt in tots]

    order = range(nchunk - 1, -1, -1) if reverse else range(nchunk)
    for h in range(GLA_HEADS):
        yield
        hs = slice(h * GLA_DK, (h + 1) * GLA_DK)
        vs = slice(h * GLA_DV, (h + 1) * GLA_DV)
        q = q_ref[:, hs].astype(F32) * (GLA_DK ** -0.5)
        k = k_ref[:, hs].astype(F32)
        v = v_ref[:, vs]
        qd = (q * e_cum[:, hs]).astype(BF16)
        ki = (k * e_inv[:, hs]).astype(BF16)
        ke = (k * e_end[:, hs]).astype(BF16)
        att = lax.dot_general(qd, ki, NT_DIMS, preferred_element_type=F32)
        o_intra = jnp.dot(jnp.where(keep, att, 0.0).astype(BF16), v, preferred_element_type=F32)
        st = st_ref[h]
        o_inter = [None] * nchunk
        for ci in order:
            rows = slice(ci * c, (ci + 1) * c)
            o_inter[ci] = lax.dot_general(qd[rows], st.astype(BF16), NT_DIMS, preferred_element_type=F32)
            upd = lax.dot_general(v[rows], ke[rows], (((0,), (0,)), ((), ())), preferred_element_type=F32)
            st = st * e_tot[ci][:, hs] + upd
        st_ref[h] = st
        o_ref[:, vs] = (o_intra + jnp.concatenate(o_inter, axis=0)).astype(o_ref.dtype)


def _time_block_index(t, nblk, reverse):
    if not reverse:
        return t
    return jnp.where(t == 0, 0, nblk - t)


def _gla_call(p, wd_pad, bd, *, batch, t_len):
    m = p.shape[0]
    tb = TIME_BLOCK
    nblk = t_len // tb

    def blocks(reverse):
        def rowblk(b, t):
            return b * nblk + _time_block_index(t, nblk, reverse)
        return [pl.BlockSpec((tb, GLA_QK), lambda b, t: (rowblk(b, t), COL_GQ // GLA_QK)),
                pl.BlockSpec((tb, GLA_QK), lambda b, t: (rowblk(b, t), COL_GK // GLA_QK)),
                pl.BlockSpec((tb, GLA_V), lambda b, t: (rowblk(b, t), COL_GV // GLA_V)),
                pl.BlockSpec((tb, LANES), lambda b, t: (rowblk(b, t), COL_LR // LANES))], \
            pl.BlockSpec((tb, GLA_V), lambda b, t: (rowblk(b, t), 0))

    in_f, out_f = blocks(False)
    in_b, out_b = blocks(True)
    state = pltpu.VMEM((GLA_HEADS, GLA_DV, GLA_DK), F32)
    return pl.pallas_call(
        functools.partial(_gla_kernel, tb=tb),
        grid=(batch, nblk),
        in_specs=in_f + in_b + [pl.BlockSpec((2, 2, LANES, GLA_QK), lambda b, t: (0, 0, 0, 0)),
                                pl.BlockSpec((2, 1, GLA_QK), lambda b, t: (0, 0, 0))],
        out_specs=[out_f, out_b],
        out_shape=[jax.ShapeDtypeStruct((m, GLA_V), BF16)] * 2,
        scratch_shapes=[state, state],
        compiler_params=_cparams(("parallel", "arbitrary")),
        name="gla_scan",
    )(p, p, p, p, p, p, p, p, wd_pad, bd)


Q_SCALE = HEAD_DIM ** -0.5 * LOG2_E


def _rope_prep(x, g, cos, sin, scale):
    lane = lax.broadcasted_iota(jnp.int32, (1, HEAD_DIM), 1)
    first = (lane % (HEAD_DIM // 2)) < (HEAD_DIM // 4)
    y = _rms(x.astype(F32), g)
    partner = jnp.where(first, pltpu.roll(y, HEAD_DIM - HEAD_DIM // 4, 1), pltpu.roll(y, HEAD_DIM // 4, 1))
    out = y * cos + partner * sin
    if scale != 1.0:
        out = out * scale
    return out.astype(BF16)


def _qkprep_kernel(q_ref, k_ref, v_ref, cos_ref, sin_ref, gq_ref, gk_ref, qo_ref, ko_ref, vo_ref):
    for h in range(ATT_Q_HEADS):
        hs = slice(h * HEAD_DIM, (h + 1) * HEAD_DIM)
        qo_ref[:, hs] = _rope_prep(q_ref[:, hs], gq_ref[...], cos_ref[...], sin_ref[...], Q_SCALE)
    for h in range(ATT_KV_HEADS):
        hs = slice(h * HEAD_DIM, (h + 1) * HEAD_DIM)
        ko_ref[:, hs] = _rope_prep(k_ref[:, hs], gk_ref[...], cos_ref[...], sin_ref[...], 1.0)
        vo_ref[:, 2 * h * HEAD_DIM:(2 * h + 1) * HEAD_DIM] = v_ref[:, hs]
        vo_ref[:, (2 * h + 1) * HEAD_DIM:(2 * h + 2) * HEAD_DIM] = jnp.ones((v_ref.shape[0], HEAD_DIM), BF16)


def _qkprep_call(p, cos_t, sin_t, gq, gk, *, t_len):
    m = p.shape[0]
    tb = TIME_BLOCK
    nblk = t_len // tb
    return pl.pallas_call(
        _qkprep_kernel,
        grid=(m // tb,),
        in_specs=[pl.BlockSpec((tb, ATT_Q), lambda i: (i, COL_AQ // ATT_Q)),
                  pl.BlockSpec((tb, ATT_KV), lambda i: (i, COL_AK // ATT_KV)),
                  pl.BlockSpec((tb, ATT_KV), lambda i: (i, COL_AV // ATT_KV)),
                  pl.BlockSpec((tb, HEAD_DIM), lambda i: (i % nblk, 0)),
                  pl.BlockSpec((tb, HEAD_DIM), lambda i: (i % nblk, 0)),
                  pl.BlockSpec((1, HEAD_DIM), lambda i: (0, 0)),
                  pl.BlockSpec((1, HEAD_DIM), lambda i: (0, 0))],
        out_specs=[pl.BlockSpec((tb, ATT_Q), lambda i: (i, 0)),
                   pl.BlockSpec((tb, ATT_KV), lambda i: (i, 0)),
                   pl.BlockSpec((tb, 2 * ATT_KV), lambda i: (i, 0))],
        out_shape=[jax.ShapeDtypeStruct((m, ATT_Q), BF16), jax.ShapeDtypeStruct((m, ATT_KV), BF16),
                   jax.ShapeDtypeStruct((m, 2 * ATT_KV), BF16)],
        compiler_params=_cparams(("parallel",)),
        name="qk_prep",
    )(p, p, p, cos_t, sin_t, gq, gk)


NEG_BIG = -1e30


ATT_ROW_SUB = 64
ATT_HEADS_PER_STEP = 4
NT_DIMS = (((1,), (1,)), ((), ()))


def _attn_kernel(q_ref, k_ref, v_ref, o_ref, s_ref, p_ref, a_ref, m_ref, acc_ref, *, tq, tk, t_len):
    nkv = t_len // tk
    sub = ATT_ROW_SUB
    nlt = tk // LANES
    nh = q_ref.shape[1] // HEAD_DIM
    m_ref[...] = jnp.full_like(m_ref, NEG_BIG)
    acc_ref[...] = jnp.zeros_like(acc_ref)
    items = [(h, j) for h in range(nh) for j in range(nkv)]

    def scores(i):
        h, j = items[i]
        q = q_ref[:, h * HEAD_DIM:(h + 1) * HEAD_DIM]
        s_ref[i % 2] = lax.dot_general(q, k_ref[j * tk:(j + 1) * tk, :], NT_DIMS, preferred_element_type=F32)

    def exponentials(i):
        h, _ = items[i]
        slot = i % 2
        for r in range(tq // sub):
            rows = slice(r * sub, (r + 1) * sub)
            tiles = [s_ref[slot, rows, t * LANES:(t + 1) * LANES] for t in range(nlt)]
            mx = tiles[0]
            for t in range(1, nlt):
                mx = jnp.maximum(mx, tiles[t])
            m_prev = m_ref[h, rows, :]
            m_new = jnp.maximum(m_prev, jnp.max(mx, axis=1, keepdims=True))
            a_ref[slot, rows, :] = jnp.exp2(m_prev - m_new)
            m_ref[h, rows, :] = m_new
            for t in range(nlt):
                p_ref[slot, rows, t * LANES:(t + 1) * LANES] = jnp.exp2(tiles[t] - m_new).astype(BF16)

    def weighted_values(i):
        h, j = items[i]
        slot = i % 2
        pv = jnp.dot(p_ref[slot], v_ref[j * tk:(j + 1) * tk, :], preferred_element_type=F32)
        alpha = a_ref[slot]
        acc_ref[h] = jnp.concatenate([alpha, alpha], axis=1) * acc_ref[h] + pv

    n = len(items)
    for step in range(n + 2):
        if step < n:
            scores(step)
        if 1 <= step <= n:
            exponentials(step - 1)
        if step >= 2:
            weighted_values(step - 2)

    for h in range(nh):
        acc = acc_ref[h]
        o_ref[:, h * HEAD_DIM:(h + 1) * HEAD_DIM] = (acc[:, :HEAD_DIM] / acc[:, HEAD_DIM:]).astype(o_ref.dtype)


def _attn_call(qn, kn, vext, *, batch, t_len, tq, tk):
    m = qn.shape[0]
    nq = t_len // tq
    assert tq % ATT_ROW_SUB == 0 and tk % LANES == 0 and t_len % tk == 0
    kern = functools.partial(_attn_kernel, tq=tq, tk=tk, t_len=t_len)
    nh = ATT_HEADS_PER_STEP
    hsteps = ATT_GROUP // nh
    width = nh * HEAD_DIM
    return pl.pallas_call(
        kern,
        grid=(batch, ATT_KV_HEADS, nq, hsteps),
        in_specs=[pl.BlockSpec((tq, width), lambda b, g, i, h: (b * nq + i, g * hsteps + h)),
                  pl.BlockSpec((t_len, HEAD_DIM), lambda b, g, i, h: (b, g)),
                  pl.BlockSpec((t_len, 2 * HEAD_DIM), lambda b, g, i, h: (b, g))],
        out_specs=pl.BlockSpec((tq, width), lambda b, g, i, h: (b * nq + i, g * hsteps + h)),
        out_shape=jax.ShapeDtypeStruct((m, ATT_Q), BF16),
        scratch_shapes=[pltpu.VMEM((2, tq, tk), F32), pltpu.VMEM((2, tq, tk), BF16),
                        pltpu.VMEM((2, tq, LANES), F32), pltpu.VMEM((nh, tq, LANES), F32),
                        pltpu.VMEM((nh, tq, 2 * HEAD_DIM), F32)],
        compiler_params=_cparams(("parallel", "parallel", "arbitrary", "arbitrary")),
        name="flash_attn",
    )(qn, kn, vext)


def _attn_ctx_kernel(q_ref, k_ref, v_ref, att_ref, o_ref):
    del att_ref
    k = k_ref[...]
    v = v_ref[...]
    for h in range(ATT_GROUP):
        hs = slice(h * HEAD_DIM, (h + 1) * HEAD_DIM)
        s = lax.dot_general(q_ref[:, hs], k, NT_DIMS, preferred_element_type=F32)
        pexp = jnp.exp2(s - jnp.max(s, axis=1, keepdims=True)).astype(BF16)
        acc = jnp.dot(pexp, v, preferred_element_type=F32)
        o_ref[:, hs] = (acc[:, :HEAD_DIM] / acc[:, HEAD_DIM:]).astype(o_ref.dtype)


def _attn_ctx_call(qn, kn, vext, att, *, batch, t_len, nctx):
    nblk = t_len // nctx
    width = ATT_GROUP * HEAD_DIM
    return pl.pallas_call(
        _attn_ctx_kernel,
        grid=(batch, ATT_KV_HEADS),
        in_specs=[pl.BlockSpec((nctx, width), lambda b, g: (b * nblk, g)),
                  pl.BlockSpec((nctx, HEAD_DIM), lambda b, g: (b * nblk, g)),
                  pl.BlockSpec((nctx, 2 * HEAD_DIM), lambda b, g: (b * nblk, g)),
                  pl.BlockSpec(memory_space=pl.ANY)],
        out_specs=pl.BlockSpec((nctx, width), lambda b, g: (b * nblk, g)),
        out_shape=jax.ShapeDtypeStruct(att.shape, att.dtype),
        input_output_aliases={3: 0},
        compiler_params=_cparams(("parallel", "parallel")),
        name="ctx_attn",
    )(qn, kn, vext, att)


def _halo_valid(blk, nblk):
    prev_ok = blk >= 2
    next_ok = jnp.logical_and(blk != 0, blk != nblk - 1)
    return prev_ok, next_ok


def _shift_matrices(tb, offsets):
    r = jnp.arange(tb)[:, None]
    c = jnp.arange(tb)[None, :]
    return jnp.stack([(c == r + off) for off in offsets]).astype(BF16)


def _lru_kernel(x_ref, xp_ref, xn_ref, sh_ref, cw_ref, cb_ref, wg_ref, bg_ref, lam_ref, h_ref,
                carry_ref, a_ref, u_ref, hs_ref, hl_ref, ac_ref, *, reverse, tb, nblk):
    t = pl.program_id(1)
    blk = _time_block_index(t, nblk, reverse)

    @pl.when(t == 0)
    def _():
        carry_ref[...] = jnp.zeros_like(carry_ref)

    prev_ok, next_ok = _halo_valid(blk, nblk)
    xb = x_ref[...]
    x = xb.astype(F32)
    prev = jnp.where(prev_ok, xp_ref[...].astype(F32), 0.0)
    nxt = jnp.where(next_ok, xn_ref[...].astype(F32), 0.0)
    xm2 = jnp.dot(sh_ref[0], xb, preferred_element_type=F32)
    xm1 = jnp.dot(sh_ref[1], xb, preferred_element_type=F32)
    xp1 = jnp.dot(sh_ref[2], xb, preferred_element_type=F32)
    xc = cw_ref[0:1] * xm2 + cw_ref[1:2] * xm1 + cw_ref[2:3] * x + cw_ref[3:4] * xp1 + cb_ref[...]
    r8 = lax.broadcasted_iota(jnp.int32, (SUBLANES, 1), 0)
    p6 = prev[SUBLANES - 2:SUBLANES - 1]
    p7 = prev[SUBLANES - 1:SUBLANES]
    fix_first = jnp.where(r8 == 0, cw_ref[0:1] * p6 + cw_ref[1:2] * p7, jnp.where(r8 == 1, cw_ref[0:1] * p7, 0.0))
    fix_last = jnp.where(r8 == SUBLANES - 1, cw_ref[3:4] * nxt[0:1], 0.0)
    xc = jnp.concatenate([xc[:SUBLANES] + fix_first, xc[SUBLANES:tb - SUBLANES], xc[tb - SUBLANES:] + fix_last],
                         axis=0)

    xcb = xc.astype(BF16)
    zs = []
    for gi in range(2):
        parts = [jnp.dot(xcb[:, hb * LRU_BW:(hb + 1) * LRU_BW], wg_ref[0, gi, hb], preferred_element_type=F32)
                 for hb in range(LRU_BLOCKS)]
        zs.append(jnp.concatenate(parts, axis=-1) + bg_ref[0, gi:gi + 1, :])
    gate_r = _sigmoid(zs[0])
    gate_i = _sigmoid(zs[1])
    neg_lam = -lam_ref[0]
    softplus = jnp.maximum(neg_lam, 0.0) + jnp.log(1.0 + jnp.exp(-jnp.abs(neg_lam)))
    log_a = (-LRU_C) * gate_r * softplus
    a = jnp.exp(log_a)
    one_m_a2 = jnp.tanh(-log_a) * (1.0 + a * a)
    root = jnp.where(one_m_a2 > 0.0, one_m_a2 * lax.rsqrt(one_m_a2), 0.0)
    u = root * (gate_i * xc)

    ncol = LRU_WIDTH // LANES
    for cb in range(ncol):
        a_ref[cb] = a[:, cb * LANES:(cb + 1) * LANES]
        u_ref[cb] = u[:, cb * LANES:(cb + 1) * LANES]
    seg = LRU_SEG
    grp_rows = SUBLANES * seg
    ngrp = tb // grp_rows

    def tile_rows(grp, j):
        return pl.ds(grp * grp_rows + j, SUBLANES, stride=seg)

    def strided(ref, grp, j):
        return jnp.concatenate([ref[cb, tile_rows(grp, j), :] for cb in range(ncol)], axis=1)

    steps = range(seg - 1, -1, -1) if reverse else range(seg)
    summaries = []
    for grp in range(ngrp):
        h_loc = jnp.zeros((SUBLANES, LRU_WIDTH), F32)
        a_cum = jnp.ones((SUBLANES, LRU_WIDTH), F32)
        for j in steps:
            a_j = strided(a_ref, grp, j)
            u_j = strided(u_ref, grp, j)
            h_loc = a_j * h_loc + u_j
            a_cum = a_j * a_cum
            rows = slice((grp * seg + j) * SUBLANES, (grp * seg + j + 1) * SUBLANES)
            hl_ref[rows, :] = h_loc
            ac_ref[rows, :] = a_cum
        e, pr = h_loc, a_cum
        for sft in (1, 2, 4):
            if reverse:
                e_sh, p_sh = pltpu.roll(e, SUBLANES - sft, 0), pltpu.roll(pr, SUBLANES - sft, 0)
                valid = r8 < SUBLANES - sft
            else:
                e_sh, p_sh, valid = pltpu.roll(e, sft, 0), pltpu.roll(pr, sft, 0), r8 >= sft
            e = jnp.where(valid, e + pr * e_sh, e)
            pr = jnp.where(valid, pr * p_sh, pr)
        summaries.append((e, pr))

    c0 = carry_ref[...]
    c_in = [None] * ngrp
    for grp in (range(ngrp - 1, -1, -1) if reverse else range(ngrp)):
        e, pr = summaries[grp]
        g = e + pr * c0
        if reverse:
            c_in[grp] = jnp.where(r8 == SUBLANES - 1, c0, pltpu.roll(g, SUBLANES - 1, 0))
            c0 = jnp.broadcast_to(g[0:1], (SUBLANES, LRU_WIDTH))
        else:
            c_in[grp] = jnp.where(r8 == 0, c0, pltpu.roll(g, 1, 0))
            c0 = jnp.broadcast_to(g[SUBLANES - 1:SUBLANES], (SUBLANES, LRU_WIDTH))
    carry_ref[...] = c0

    for grp in range(ngrp):
        for j in range(seg):
            rows = slice((grp * seg + j) * SUBLANES, (grp * seg + j + 1) * SUBLANES)
            h_j = hl_ref[rows, :] + ac_ref[rows, :] * c_in[grp]
            for cb in range(ncol):
                hs_ref[cb, tile_rows(grp, j), :] = h_j[:, cb * LANES:(cb + 1) * LANES]
    for cb in range(ncol):
        h_ref[:, cb * LANES:(cb + 1) * LANES] = hs_ref[cb].astype(h_ref.dtype)


def _lru_call(p, cw, cb, wg, bg, lam, *, reverse, batch, t_len):
    m = p.shape[0]
    tb = TIME_BLOCK
    nblk = t_len // tb
    per = tb // SUBLANES
    nrow8 = m // SUBLANES
    d = 1 if reverse else 0

    def rowblk(b, t):
        return b * nblk + _time_block_index(t, nblk, reverse)

    kern = functools.partial(_lru_kernel, reverse=reverse, tb=tb, nblk=nblk)
    return pl.pallas_call(
        kern,
        grid=(batch, nblk),
        in_specs=[pl.BlockSpec((tb, LRU_WIDTH), lambda b, t: (rowblk(b, t), COL_LX // LRU_WIDTH)),
                  pl.BlockSpec((SUBLANES, LRU_WIDTH),
                               lambda b, t: (jnp.maximum(rowblk(b, t) * per - 1, 0), COL_LX // LRU_WIDTH)),
                  pl.BlockSpec((SUBLANES, LRU_WIDTH),
                               lambda b, t: (jnp.minimum((rowblk(b, t) + 1) * per, nrow8 - 1), COL_LX // LRU_WIDTH)),
                  pl.BlockSpec((LRU_CONV - 1, tb, tb), lambda b, t: (0, 0, 0)),
                  pl.BlockSpec((LRU_CONV, LRU_WIDTH), lambda b, t: (0, 0)),
                  pl.BlockSpec((1, LRU_WIDTH), lambda b, t: (0, 0)),
                  pl.BlockSpec((1, 2, LRU_BLOCKS, LRU_BW, LRU_BW), lambda b, t: (d, 0, 0, 0, 0)),
                  pl.BlockSpec((1, 2, LRU_WIDTH), lambda b, t: (d, 0, 0)),
                  pl.BlockSpec((1, 1, LRU_WIDTH), lambda b, t: (d, 0, 0))],
        out_specs=pl.BlockSpec((tb, LRU_WIDTH), lambda b, t: (rowblk(b, t), 0)),
        out_shape=jax.ShapeDtypeStruct((m, LRU_WIDTH), BF16),
        scratch_shapes=([pltpu.VMEM((SUBLANES, LRU_WIDTH), F32)]
                        + [pltpu.VMEM((LRU_WIDTH // LANES, tb, LANES), F32)] * 3
                        + [pltpu.VMEM((tb, LRU_WIDTH), F32)] * 2),
        compiler_params=_cparams(("parallel", "arbitrary")),
        name="lru_bwd" if reverse else "lru_fwd",
    )(p, p, p, _shift_matrices(tb, (-2, -1, 1)), cw, cb, wg, bg, lam)


def _select_mod(mod_ref, idx, i, tm, tiles_per_batch, nctx):
    row = (i % tiles_per_batch) * tm + lax.broadcasted_iota(jnp.int32, (tm, 1), 0)
    return jnp.where(row < nctx, mod_ref[0, idx:idx + 1, :], mod_ref[0, 6 + idx:7 + idx, :])


def _merge_kernel(of_ref, ob_ref, r_ref, att_ref, hf_ref, hb_ref, y_ref, gate_ref, x_ref, mod_ref,
                  gn_ref, gp_ref, wb_ref, wo_ref, o_ref, *, tm, tiles_per_batch, nctx):
    i = pl.program_id(0)
    o = of_ref[...].astype(F32) + ob_ref[...].astype(F32)
    gla_parts = []
    for h in range(GLA_HEADS):
        vs = slice(h * GLA_DV, (h + 1) * GLA_DV)
        gla_parts.append(_rms(o[:, vs], gn_ref[:, vs]))
    gla = jnp.concatenate(gla_parts, axis=-1) * _silu(r_ref[...].astype(F32))
    lru = (hf_ref[...].astype(F32) + hb_ref[...].astype(F32)) * _gelu_tanh(y_ref[...].astype(F32))
    branches = (gla.astype(BF16), att_ref[...], lru.astype(BF16))
    mixed = None
    for bi in range(N_BRANCHES):
        gate = _sigmoid(gate_ref[:, bi * D_MODEL:(bi + 1) * D_MODEL].astype(F32))
        term = gate * jnp.dot(branches[bi], wb_ref[bi], preferred_element_type=F32)
        mixed = term if mixed is None else mixed + term
    y = jnp.dot(mixed.astype(BF16), wo_ref[...], preferred_element_type=F32)
    gate_vec = _select_mod(mod_ref, 2, i, tm, tiles_per_batch, nctx)
    o_ref[...] = x_ref[...] + gate_vec * _rms(y, gp_ref[...])


def _merge_call(o_f, o_b, p, att, h_f, h_b, x, modsel, gn, gp, wb, wo, *, tm, t_len, nctx):
    m, d = x.shape
    tiles_per_batch = t_len // tm
    kern = functools.partial(_merge_kernel, tm=tm, tiles_per_batch=tiles_per_batch, nctx=nctx)
    row = lambda i: (i, 0)
    const2 = lambda i: (0, 0)
    return pl.pallas_call(
        kern,
        grid=(m // tm,),
        in_specs=[pl.BlockSpec((tm, d), row),
                  pl.BlockSpec((tm, d), row),
                  pl.BlockSpec((tm, d), lambda i: (i, COL_GR // D_MODEL)),
                  pl.BlockSpec((tm, d), row),
                  pl.BlockSpec((tm, d), row),
                  pl.BlockSpec((tm, d), row),
                  pl.BlockSpec((tm, d), lambda i: (i, COL_LY // D_MODEL)),
                  pl.BlockSpec((tm, N_BRANCHES * d), lambda i: (i, COL_GATE // (N_BRANCHES * D_MODEL))),
                  pl.BlockSpec((tm, d), row),
                  pl.BlockSpec((1, 12, d), lambda i: (i // tiles_per_batch, 0, 0)),
                  pl.BlockSpec((1, d), const2),
                  pl.BlockSpec((1, d), const2),
                  pl.BlockSpec((N_BRANCHES, d, d), lambda i: (0, 0, 0)),
                  pl.BlockSpec((d, d), const2)],
        out_specs=pl.BlockSpec((tm, d), row),
        out_shape=jax.ShapeDtypeStruct((m, d), F32),
        compiler_params=_cparams(("parallel",)),
        name="merge_out",
    )(o_f, o_b, p, att, h_f, h_b, p, p, x, modsel, gn, gp, wb, wo)


def _ffn_down_kernel(u_ref, up_ref, un_ref, cw_ref, cb_ref, wd_ref, x_ref, mod_ref, gp_ref, o_ref,
                     *, tb, nblk):
    i = pl.program_id(0)
    blk = i % nblk
    prev_ok, next_ok = _halo_valid(blk, nblk)
    r8 = lax.broadcasted_iota(jnp.int32, (SUBLANES, 1), 0)

    def conv(cols):
        u = u_ref[:, cols].astype(F32)
        prev = jnp.where(prev_ok, up_ref[SUBLANES - 1:SUBLANES, cols].astype(F32), 0.0)
        nxt = jnp.where(next_ok, un_ref[0:1, cols].astype(F32), 0.0)
        um1 = pltpu.roll(u, 1, 0)
        up1 = pltpu.roll(u, tb - 1, 0)
        um1 = jnp.concatenate([jnp.where(r8 == 0, prev, um1[:SUBLANES]), um1[SUBLANES:]], axis=0)
        up1 = jnp.concatenate([up1[:tb - SUBLANES], jnp.where(r8 == SUBLANES - 1, nxt, up1[tb - SUBLANES:])], axis=0)
        return cw_ref[0:1, cols] * um1 + cw_ref[1:2, cols] * u + cw_ref[2:3, cols] * up1 + cb_ref[:, cols]

    bounds = list(range(0, D_FF, FFN_COL_CHUNK)) + [D_FF]
    y = None
    for lo, hi in zip(bounds[:-1], bounds[1:]):
        act = (_silu(conv(slice(lo, hi))) * conv(slice(D_FF + lo, D_FF + hi))).astype(BF16)
        part = jnp.dot(act, wd_ref[lo:hi, :], preferred_element_type=F32)
        y = part if y is None else y + part
    gate_vec = _select_mod(mod_ref, 5, i, tb, nblk, tb)
    o_ref[...] = x_ref[...] + gate_vec * _rms(y, gp_ref[...])


def _ffn_down_call(u, cw, cb, wd, x, modsel, gp, *, t_len, latent_only):
    m, d = x.shape
    tb = TIME_BLOCK
    nblk = t_len // tb
    per = tb // SUBLANES
    nrow8 = m // SUBLANES
    kern = functools.partial(_ffn_down_kernel, tb=tb, nblk=nblk)
    if latent_only:
        out_rows = m // nblk * (nblk - 1)
        out_map = lambda i: ((i // nblk) * (nblk - 1) + jnp.maximum(i % nblk - 1, 0), 0)
    else:
        out_rows = m
        out_map = lambda i: (i, 0)
    return pl.pallas_call(
        kern,
        grid=(m // tb,),
        in_specs=[pl.BlockSpec((tb, 2 * D_FF), lambda i: (i, 0)),
                  pl.BlockSpec((SUBLANES, 2 * D_FF), lambda i: (jnp.maximum(i * per - 1, 0), 0)),
                  pl.BlockSpec((SUBLANES, 2 * D_FF), lambda i: (jnp.minimum((i + 1) * per, nrow8 - 1), 0)),
                  pl.BlockSpec((FFN_CONV, 2 * D_FF), lambda i: (0, 0)),
                  pl.BlockSpec((1, 2 * D_FF), lambda i: (0, 0)),
                  pl.BlockSpec((D_FF, d), lambda i: (0, 0)),
                  pl.BlockSpec((tb, d), lambda i: (i, 0)),
                  pl.BlockSpec((1, 12, d), lambda i: (i // nblk, 0, 0)),
                  pl.BlockSpec((1, d), lambda i: (0, 0))],
        out_specs=pl.BlockSpec((tb, d), out_map),
        out_shape=jax.ShapeDtypeStruct((out_rows, d), F32),
        compiler_params=_cparams(("arbitrary",)),
        name="ffn_down",
    )(u, u, u, cw, cb, wd, x, modsel, gp)


def _rope_tables(seq, nctx):
    rows = seq // GRID_W
    row = np.repeat(np.arange(rows), GRID_W)
    col = np.tile(np.arange(GRID_W), rows)
    axis_dim = HEAD_DIM // 2
    inv_freq = (1.0 / (np.float32(ROPE_THETA) ** (np.arange(0, axis_dim, 2, dtype=np.float32) / np.float32(axis_dim))))
    ang = np.stack([row, col], axis=-1).astype(np.float32)[:, :, None] * inv_freq.astype(np.float32)
    cos, sin = np.cos(ang), np.sin(ang)
    cos_t = np.concatenate([cos[:, 0], cos[:, 0], cos[:, 1], cos[:, 1]], axis=-1)
    sin_t = np.concatenate([-sin[:, 0], sin[:, 0], -sin[:, 1], sin[:, 1]], axis=-1)
    cos_t = np.concatenate([np.ones((nctx, HEAD_DIM), np.float32), cos_t], axis=0)
    sin_t = np.concatenate([np.zeros((nctx, HEAD_DIM), np.float32), sin_t], axis=0)
    return jnp.asarray(cos_t, F32), jnp.asarray(sin_t, F32)


def _permute_w_in(w):
    offs = [0]
    for s in IN_SPLITS:
        offs.append(offs[-1] + s)
    seg = lambda i: w[:, offs[i]:offs[i + 1]].astype(BF16)
    gq, gk, gv, gr, lrf, lrb, aq, ak, av, lx, ly, gates = (seg(i) for i in range(12))
    pad = jnp.zeros((w.shape[0], D_IN_PAD - COL_LR - 2 * GLA_RANK), BF16)
    return jnp.concatenate([gq, gk, gv, gr, aq, lx, ly, gates, ak, av, lrf, lrb, pad], axis=1)


def _largest_tile(t_len, cap):
    best = SUBLANES
    for cand in range(SUBLANES, cap + 1, SUBLANES):
        if t_len % cand == 0:
            best = cand
    return best


def kernel(x, c, ctx, c_ctx, w_ada, b_ada, norm_gains, w_in, gla_w_decay, gla_b_decay, gla_norm_g,
           att_q_norm_g, att_k_norm_g, lru_conv_w, lru_conv_b, lru_w_gates, lru_b_gates, lru_lambda,
           w_branch, w_out, ffn_w_up, ffn_conv_w, ffn_conv_b, ffn_w_down):
    batch, seq, d = x.shape
    nctx = ctx.shape[1]
    depth = w_ada.shape[0]
    assert d == D_MODEL and nctx == TIME_BLOCK and seq % TIME_BLOCK == 0 and seq % GRID_W == 0
    t_len = nctx + seq
    m = batch * t_len
    tm_big = _largest_tile(t_len, 1408)
    tq = _largest_tile(t_len, 768)
    tk = _largest_tile(t_len, 768)
    assert tm_big % TIME_BLOCK == 0 or tm_big % LANES == 0
    assert tq % LANES == 0 and tk % TIME_BLOCK == 0

    xc = jnp.concatenate([ctx, x], axis=1).reshape(m, d)

    rows = -(-(batch + 1) // SUBLANES) * SUBLANES
    cvec = jnp.zeros((rows, d), F32).at[:batch].set(c).at[batch].set(c_ctx)
    mod = _ada_call(cvec, w_ada, b_ada)
    mod = mod.reshape(depth, rows, 6, d)
    cos_t, sin_t = _rope_tables(seq, nctx)

    for l in range(depth):
        mod_c = jnp.broadcast_to(mod[l, batch][None], (batch, 6, d))
        modsel = jnp.concatenate([mod_c, mod[l, :batch]], axis=1)
        g_pre_mix, g_post_mix, g_pre_ffn, g_post_ffn = (norm_gains[l, i].reshape(1, d) for i in range(4))

        p = _nmm_call(xc, g_pre_mix, modsel, _permute_w_in(w_in[l]), shift_idx=0, scale_idx=1,
                      tm=tm_big, tn=1024, t_len=t_len, nctx=nctx, name="in_proj")

        wd_pad = jnp.zeros((2, LANES, GLA_QK), F32)
        wd_pad = wd_pad.at[0, :GLA_RANK].set(gla_w_decay[l, 0]).at[1, GLA_RANK:2 * GLA_RANK].set(gla_w_decay[l, 1])
        wd_pad = jnp.stack(_split_bf16(wd_pad), axis=1)
        bd = gla_b_decay[l].reshape(2, 1, GLA_QK)
        o_f, o_b = _gla_call(p, wd_pad, bd, batch=batch, t_len=t_len)

        qn, kn, vext = _qkprep_call(p, cos_t, sin_t, att_q_norm_g[l].reshape(1, HEAD_DIM),
                                    att_k_norm_g[l].reshape(1, HEAD_DIM), t_len=t_len)
        att = _attn_call(qn, kn, vext, batch=batch, t_len=t_len, tq=tq, tk=tk)
        if l < depth - 1:
            att = _attn_ctx_call(qn, kn, vext, att, batch=batch, t_len=t_len, nctx=nctx)

        wg = lru_w_gates[l].astype(BF16)
        lam = lru_lambda[l].reshape(2, 1, LRU_WIDTH)
        cb = lru_conv_b[l].reshape(1, LRU_WIDTH)
        h_f = _lru_call(p, lru_conv_w[l], cb, wg, lru_b_gates[l], lam, reverse=False, batch=batch, t_len=t_len)
        h_b = _lru_call(p, lru_conv_w[l], cb, wg, lru_b_gates[l], lam, reverse=True, batch=batch, t_len=t_len)

        xc = _merge_call(o_f, o_b, p, att, h_f, h_b, xc, modsel, gla_norm_g[l].reshape(1, GLA_V), g_post_mix,
                         w_branch[l].astype(BF16), w_out[l].astype(BF16), tm=_largest_tile(t_len, 384),
                         t_len=t_len, nctx=nctx)

        u = _nmm_call(xc, g_pre_ffn, modsel, ffn_w_up[l].astype(BF16), shift_idx=3, scale_idx=4,
                      tm=_largest_tile(t_len, 704), tn=D_FF, t_len=t_len, nctx=nctx, name="ffn_up")
        xc = _ffn_down_call(u, ffn_conv_w[l], ffn_conv_b[l].reshape(1, 2 * D_FF), ffn_w_down[l].astype(BF16),
                            xc, modsel, g_post_ffn, t_len=t_len, latent_only=(l == depth - 1))

    return xc.reshape(batch, seq, d)
```

```python
import functools

import jax
import jax.numpy as jnp
import numpy as np
from jax import lax
from jax.experimental import pallas as pl
from jax.experimental.pallas import tpu as pltpu

F32 = jnp.float32
BF16 = jnp.bfloat16

D_MODEL = 1024
NORM_EPS = 1e-6
N_BRANCHES = 3
GRID_W = 64

GLA_HEADS = 4
GLA_DK = D_MODEL // (2 * GLA_HEADS)
GLA_DV = D_MODEL // GLA_HEADS
GLA_QK = GLA_HEADS * GLA_DK
GLA_V = GLA_HEADS * GLA_DV
GLA_RANK = 16
GLA_TEMP = 16.0
GLA_CHUNK = 64
GLA_PHASE_LAG = 3

HEAD_DIM = 128
ATT_Q_HEADS = D_MODEL // HEAD_DIM
ATT_KV_HEADS = ATT_Q_HEADS // 4
ATT_GROUP = ATT_Q_HEADS // ATT_KV_HEADS
ATT_Q = ATT_Q_HEADS * HEAD_DIM
ATT_KV = ATT_KV_HEADS * HEAD_DIM
ROPE_THETA = 10000.0
LOG2_E = 1.4426950408889634

LRU_WIDTH = D_MODEL
LRU_BLOCKS = 8
LRU_BW = LRU_WIDTH // LRU_BLOCKS
LRU_C = 8.0
LRU_CONV = 4
LRU_CONV_LEFT = 2
LRU_SEG = 4

D_FF = 2816
FFN_CONV = 3
FFN_COL_CHUNK = 256

IN_SPLITS = (GLA_QK, GLA_QK, GLA_V, GLA_V, GLA_RANK, GLA_RANK,
             ATT_Q, ATT_KV, ATT_KV, LRU_WIDTH, LRU_WIDTH, N_BRANCHES * D_MODEL)

SUBLANES = 8
LANES = 128
TIME_BLOCK = 256

COL_GQ = 0
COL_GK = 512
COL_GV = 1024
COL_GR = 2048
COL_AQ = 3072
COL_LX = 4096
COL_LY = 5120
COL_GATE = 6144
COL_AK = 9216
COL_AV = 9472
COL_LR = 9728
D_IN_PAD = 10240

VMEM_LIMIT = 56 * 1024 * 1024


def _cparams(sem):
    return pltpu.CompilerParams(dimension_semantics=sem, vmem_limit_bytes=VMEM_LIMIT)


def _sigmoid(x):
    return 0.5 * jnp.tanh(0.5 * x) + 0.5


def _silu(x):
    hx = 0.5 * x
    return hx * jnp.tanh(hx) + hx


def _gelu_tanh(x):
    hx = 0.5 * x
    inner = x * (0.7978845608028654 + (0.7978845608028654 * 0.044715) * (x * x))
    return hx * jnp.tanh(inner) + hx


def _split_bf16(x):
    hi = x.astype(BF16)
    lo = (x - hi.astype(F32)).astype(BF16)
    return hi, lo


def _rms(x, g):
    ms = jnp.mean(x * x, axis=-1, keepdims=True)
    return x * lax.rsqrt(ms + NORM_EPS) * g


def _ada_kernel(c_ref, w_ref, b_ref, o_ref):
    c = c_ref[...]
    o_ref[0] = jnp.dot(_silu(c), w_ref[0], preferred_element_type=F32,
                       precision=lax.Precision.HIGHEST) + b_ref[0]


def _ada_call(cvec, w_ada, b_ada):
    depth, d, n = w_ada.shape
    tn = 1536
    rows = cvec.shape[0]
    return pl.pallas_call(
        _ada_kernel,
        grid=(depth, n // tn),
        in_specs=[pl.BlockSpec((rows, d), lambda l, j: (0, 0)),
                  pl.BlockSpec((1, d, tn), lambda l, j: (l, 0, j)),
                  pl.BlockSpec((1, 1, tn), lambda l, j: (l, 0, j))],
        out_specs=pl.BlockSpec((1, rows, tn), lambda l, j: (l, 0, j)),
        out_shape=jax.ShapeDtypeStruct((depth, rows, n), F32),
        compiler_params=_cparams(("parallel", "parallel")),
        name="ada_mod",
    )(cvec, w_ada, b_ada.reshape(depth, 1, n))


def _nmm_kernel(x_ref, g_ref, mod_ref, w_ref, o_ref, h_ref, *, shift_idx, scale_idx, tm, nctx, tiles_per_batch,
                epilogues):
    i = pl.program_id(0)

    @pl.when(pl.program_id(1) == 0)
    def _():
        y = _rms(x_ref[...], g_ref[...])
        row = (i % tiles_per_batch) * tm + lax.broadcasted_iota(jnp.int32, (tm, 1), 0)
        is_ctx = row < nctx
        shift = jnp.where(is_ctx, mod_ref[0, shift_idx:shift_idx + 1, :], mod_ref[0, 6 + shift_idx:7 + shift_idx, :])
        scale = jnp.where(is_ctx, mod_ref[0, scale_idx:scale_idx + 1, :], mod_ref[0, 6 + scale_idx:7 + scale_idx, :])
        h_ref[...] = (y * (1.0 + scale) + shift).astype(BF16)

    def emit(fn):
        acc = jnp.dot(h_ref[...], w_ref[...], preferred_element_type=F32)
        o_ref[...] = (acc if fn is None else fn(acc)).astype(o_ref.dtype)

    if not epilogues:
        emit(None)
    else:
        j = pl.program_id(1)
        plain = None
        for fn, lo, hi in epilogues:
            here = jnp.logical_and(j >= lo, j < hi)
            pl.when(here)(functools.partial(emit, fn))
            plain = jnp.logical_not(here) if plain is None else jnp.logical_and(plain, jnp.logical_not(here))
        pl.when(plain)(functools.partial(emit, None))


def _nmm_call(x, g, modsel, w, *, shift_idx, scale_idx, tm, tn, t_len, nctx, name, epilogues=()):
    m, d = x.shape
    n = w.shape[1]
    tiles_per_batch = t_len // tm
    kern = functools.partial(_nmm_kernel, shift_idx=shift_idx, scale_idx=scale_idx, tm=tm, nctx=nctx,
                             tiles_per_batch=tiles_per_batch, epilogues=tuple(epilogues))
    return pl.pallas_call(
        kern,
        grid=(m // tm, n // tn),
        in_specs=[pl.BlockSpec((tm, d), lambda i, j: (i, 0)),
                  pl.BlockSpec((1, d), lambda i, j: (0, 0)),
                  pl.BlockSpec((1, 12, d), lambda i, j: (i // tiles_per_batch, 0, 0)),
                  pl.BlockSpec((d, tn), lambda i, j: (0, j))],
        out_specs=pl.BlockSpec((tm, tn), lambda i, j: (i, j)),
        out_shape=jax.ShapeDtypeStruct((m, n), BF16),
        scratch_shapes=[pltpu.VMEM((tm, d), BF16)],
        compiler_params=_cparams(("parallel", "arbitrary")),
        name=name,
    )(x, g, modsel, w)


def _log_sigmoid(z):
    return jnp.minimum(z, 0.0) - jnp.log(1.0 + jnp.exp(-jnp.abs(z)))


def _gla_kernel(qf_ref, kf_ref, vf_ref, lrf_ref, qb_ref, kb_ref, vb_ref, lrb_ref, wd_ref, bd_ref,
                of_ref, ob_ref, stf_ref, stb_ref, *, tb):
    @pl.when(pl.program_id(1) == 0)
    def _():
        stf_ref[...] = jnp.zeros_like(stf_ref)
        stb_ref[...] = jnp.zeros_like(stb_ref)

    fwd = _gla_direction(qf_ref, kf_ref, vf_ref, lrf_ref, wd_ref, bd_ref, of_ref, stf_ref, reverse=False, tb=tb)
    bwd = _gla_direction(qb_ref, kb_ref, vb_ref, lrb_ref, wd_ref, bd_ref, ob_ref, stb_ref, reverse=True, tb=tb)
    done = object()
    running = True
    for _ in range(GLA_PHASE_LAG):
        running = next(fwd, done) is not done
    while running:
        step_f = next(fwd, done)
        step_b = next(bwd, done)
        running = step_f is not done or step_b is not done


def _gla_direction(q_ref, k_ref, v_ref, lr_ref, wd_ref, bd_ref, o_ref, st_ref, *, reverse, tb):
    c = GLA_CHUNK
    nchunk = tb // c
    d = 1 if reverse else 0

    lr = lr_ref[...]
    z = (jnp.dot(lr, wd_ref[d, 0], preferred_element_type=F32)
         + jnp.dot(lr, wd_ref[d, 1], preferred_element_type=F32) + bd_ref[d])
    yield
    log_a = _log_sigmoid(z) * (1.0 / GLA_TEMP)
    r = lax.broadcasted_iota(jnp.int32, (tb, tb), 0)
    s = lax.broadcasted_iota(jnp.int32, (tb, tb), 1)
    keep = ((r // c) == (s // c)) & ((s >= r) if reverse else (s <= r))
    tri = jnp.where(keep, 1.0, 0.0).astype(BF16)
    la_hi, la_lo = _split_bf16(log_a)
    cum = jnp.dot(tri, la_hi, preferred_element_type=F32) + jnp.dot(tri, la_lo, preferred_element_type=F32)
    yield

    last = [(ci * c) if reverse else (ci * c + c - 1) for ci in range(nchunk)]
    tots = [cum[i:i + 1] for i in last]
    tot_rows = jnp.concatenate([jnp.broadcast_to(tt, (c, GLA_QK)) for tt in tots], axis=0)
    e_cum = jnp.exp(cum)
    e_inv = jnp.exp(-cum)
    e_end = jnp.exp(tot_rows - cum)
    e_tot = [jnp.exp(tt) for tt in tots]

    order = range(nchunk - 1, -1, -1) if reverse else range(nchunk)
    for h in range(GLA_HEADS):
        yield
        hs = slice(h * GLA_DK, (h + 1) * GLA_DK)
        vs = slice(h * GLA_DV, (h + 1) * GLA_DV)
        q = q_ref[:, hs].astype(F32) * (GLA_DK ** -0.5)
        k = k_ref[:, hs].astype(F32)
        v = v_ref[:, vs]
        qd = (q * e_cum[:, hs]).astype(BF16)
        ki = (k * e_inv[:, hs]).astype(BF16)
        ke = (k * e_end[:, hs]).astype(BF16)
        att = lax.dot_general(qd, ki, NT_DIMS, preferred_element_type=F32)
        o_intra = jnp.dot(jnp.where(keep, att, 0.0).astype(BF16), v, preferred_element_type=F32)
        st = st_ref[h]
        o_inter = [None] * nchunk
        for ci in order:
            rows = slice(ci * c, (ci + 1) * c)
            o_inter[ci] = lax.dot_general(qd[rows], st.astype(BF16), NT_DIMS, preferred_element_type=F32)
            upd = lax.dot_general(v[rows], ke[rows], (((0,), (0,)), ((), ())), preferred_element_type=F32)
            st = st * e_tot[ci][:, hs] + upd
        st_ref[h] = st
        o_ref[:, vs] = (o_intra + jnp.concatenate(o_inter, axis=0)).astype(o_ref.dtype)


def _time_block_index(t, nblk, reverse):
    if not reverse:
        return t
    return jnp.where(t == 0, 0, nblk - t)


def _gla_call(p, wd_pad, bd, *, batch, t_len):
    m = p.shape[0]
    tb = TIME_BLOCK
    nblk = t_len // tb

    def blocks(reverse):
        def rowblk(b, t):
            return b * nblk + _time_block_index(t, nblk, reverse)
        return [pl.BlockSpec((tb, GLA_QK), lambda b, t: (rowblk(b, t), COL_GQ // GLA_QK)),
                pl.BlockSpec((tb, GLA_QK), lambda b, t: (rowblk(b, t), COL_GK // GLA_QK)),
                pl.BlockSpec((tb, GLA_V), lambda b, t: (rowblk(b, t), COL_GV // GLA_V)),
                pl.BlockSpec((tb, LANES), lambda b, t: (rowblk(b, t), COL_LR // LANES))], \
            pl.BlockSpec((tb, GLA_V), lambda b, t: (rowblk(b, t), 0))

    in_f, out_f = blocks(False)
    in_b, out_b = blocks(True)
    state = pltpu.VMEM((GLA_HEADS, GLA_DV, GLA_DK), F32)
    return pl.pallas_call(
        functools.partial(_gla_kernel, tb=tb),
        grid=(batch, nblk),
        in_specs=in_f + in_b + [pl.BlockSpec((2, 2, LANES, GLA_QK), lambda b, t: (0, 0, 0, 0)),
                                pl.BlockSpec((2, 1, GLA_QK), lambda b, t: (0, 0, 0))],
        out_specs=[out_f, out_b],
        out_shape=[jax.ShapeDtypeStruct((m, GLA_V), BF16)] * 2,
        scratch_shapes=[state, state],
        compiler_params=_cparams(("parallel", "arbitrary")),
        name="gla_scan",
    )(p, p, p, p, p, p, p, p, wd_pad, bd)


def _qkprep_kernel(q_ref, k_ref, v_ref, cos_ref, sin_ref, gq_ref, gk_ref, qo_ref, ko_ref, vo_ref):
    cos = cos_ref[...]
    sin = sin_ref[...]
    lane = lax.broadcasted_iota(jnp.int32, (1, HEAD_DIM), 1)
    first = (lane % (HEAD_DIM // 2)) < (HEAD_DIM // 4)

    def prep(x, g, scale):
        y = _rms(x.astype(F32), g)
        partner = jnp.where(first, pltpu.roll(y, HEAD_DIM - HEAD_DIM // 4, 1), pltpu.roll(y, HEAD_DIM // 4, 1))
        out = y * cos + partner * sin
        if scale != 1.0:
            out = out * scale
        return out.astype(BF16)

    for h in range(ATT_Q_HEADS):
        hs = slice(h * HEAD_DIM, (h + 1) * HEAD_DIM)
        qo_ref[:, hs] = prep(q_ref[:, hs], gq_ref[...], HEAD_DIM ** -0.5 * LOG2_E)
    for h in range(ATT_KV_HEADS):
        hs = slice(h * HEAD_DIM, (h + 1) * HEAD_DIM)
        ko_ref[:, hs] = prep(k_ref[:, hs], gk_ref[...], 1.0)
        vo_ref[:, 2 * h * HEAD_DIM:(2 * h + 1) * HEAD_DIM] = v_ref[:, hs]
        vo_ref[:, (2 * h + 1) * HEAD_DIM:(2 * h + 2) * HEAD_DIM] = jnp.ones((v_ref.shape[0], HEAD_DIM), BF16)


def _qkprep_call(p, cos_t, sin_t, gq, gk, *, t_len):
    m = p.shape[0]
    tb = TIME_BLOCK
    nblk = t_len // tb
    return pl.pallas_call(
        _qkprep_kernel,
        grid=(m // tb,),
        in_specs=[pl.BlockSpec((tb, ATT_Q), lambda i: (i, COL_AQ // ATT_Q)),
                  pl.BlockSpec((tb, ATT_KV), lambda i: (i, COL_AK // ATT_KV)),
                  pl.BlockSpec((tb, ATT_KV), lambda i: (i, COL_AV // ATT_KV)),
                  pl.BlockSpec((tb, HEAD_DIM), lambda i: (i % nblk, 0)),
                  pl.BlockSpec((tb, HEAD_DIM), lambda i: (i % nblk, 0)),
                  pl.BlockSpec((1, HEAD_DIM), lambda i: (0, 0)),
                  pl.BlockSpec((1, HEAD_DIM), lambda i: (0, 0))],
        out_specs=[pl.BlockSpec((tb, ATT_Q), lambda i: (i, 0)),
                   pl.BlockSpec((tb, ATT_KV), lambda i: (i, 0)),
                   pl.BlockSpec((tb, 2 * ATT_KV), lambda i: (i, 0))],
        out_shape=[jax.ShapeDtypeStruct((m, ATT_Q), BF16), jax.ShapeDtypeStruct((m, ATT_KV), BF16),
                   jax.ShapeDtypeStruct((m, 2 * ATT_KV), BF16)],
        compiler_params=_cparams(("parallel",)),
        name="qk_prep",
    )(p, p, p, cos_t, sin_t, gq, gk)


NEG_BIG = -1e30


ATT_ROW_SUB = 64
ATT_HEADS_PER_STEP = 2
NT_DIMS = (((1,), (1,)), ((), ()))


def _attn_kernel(q_ref, k_ref, v_ref, o_ref, s_ref, p_ref, a_ref, m_ref, acc_ref, *, tq, tk, t_len):
    nkv = t_len // tk
    sub = ATT_ROW_SUB
    nlt = tk // LANES
    nh = q_ref.shape[1] // HEAD_DIM
    m_ref[...] = jnp.full_like(m_ref, NEG_BIG)
    acc_ref[...] = jnp.zeros_like(acc_ref)
    items = [(h, j) for h in range(nh) for j in range(nkv)]

    def scores(i):
        h, j = items[i]
        q = q_ref[:, h * HEAD_DIM:(h + 1) * HEAD_DIM]
        s_ref[i % 2] = lax.dot_general(q, k_ref[j * tk:(j + 1) * tk, :], NT_DIMS, preferred_element_type=F32)

    def exponentials(i):
        h, _ = items[i]
        slot = i % 2
        for r in range(tq // sub):
            rows = slice(r * sub, (r + 1) * sub)
            tiles = [s_ref[slot, rows, t * LANES:(t + 1) * LANES] for t in range(nlt)]
            mx = tiles[0]
            for t in range(1, nlt):
                mx = jnp.maximum(mx, tiles[t])
            m_prev = m_ref[h, rows, :]
            m_new = jnp.maximum(m_prev, jnp.max(mx, axis=1, keepdims=True))
            a_ref[slot, rows, :] = jnp.exp2(m_prev - m_new)
            m_ref[h, rows, :] = m_new
            for t in range(nlt):
                p_ref[slot, rows, t * LANES:(t + 1) * LANES] = jnp.exp2(tiles[t] - m_new).astype(BF16)

    def weighted_values(i):
        h, j = items[i]
        slot = i % 2
        pv = jnp.dot(p_ref[slot], v_ref[j * tk:(j + 1) * tk, :], preferred_element_type=F32)
        alpha = a_ref[slot]
        acc_ref[h] = jnp.concatenate([alpha, alpha], axis=1) * acc_ref[h] + pv

    n = len(items)
    for step in range(n + 2):
        if step < n:
            scores(step)
        if 1 <= step <= n:
            exponentials(step - 1)
        if step >= 2:
            weighted_values(step - 2)

    for h in range(nh):
        acc = acc_ref[h]
        o_ref[:, h * HEAD_DIM:(h + 1) * HEAD_DIM] = (acc[:, :HEAD_DIM] / acc[:, HEAD_DIM:]).astype(o_ref.dtype)


def _attn_call(qn, kn, vext, *, batch, t_len, tq):
    m = qn.shape[0]
    nq = t_len // tq
    tk = tq
    assert tq % ATT_ROW_SUB == 0 and tk % LANES == 0
    kern = functools.partial(_attn_kernel, tq=tq, tk=tk, t_len=t_len)
    nh = ATT_HEADS_PER_STEP
    hsteps = ATT_GROUP // nh
    width = nh * HEAD_DIM
    return pl.pallas_call(
        kern,
        grid=(batch, ATT_KV_HEADS, nq, hsteps),
        in_specs=[pl.BlockSpec((tq, width), lambda b, g, i, h: (b * nq + i, g * hsteps + h)),
                  pl.BlockSpec((t_len, HEAD_DIM), lambda b, g, i, h: (b, g)),
                  pl.BlockSpec((t_len, 2 * HEAD_DIM), lambda b, g, i, h: (b, g))],
        out_specs=pl.BlockSpec((tq, width), lambda b, g, i, h: (b * nq + i, g * hsteps + h)),
        out_shape=jax.ShapeDtypeStruct((m, ATT_Q), BF16),
        scratch_shapes=[pltpu.VMEM((2, tq, tk), F32), pltpu.VMEM((2, tq, tk), BF16),
                        pltpu.VMEM((2, tq, LANES), F32), pltpu.VMEM((nh, tq, LANES), F32),
                        pltpu.VMEM((nh, tq, 2 * HEAD_DIM), F32)],
        compiler_params=_cparams(("parallel", "parallel", "arbitrary", "arbitrary")),
        name="flash_attn",
    )(qn, kn, vext)


def _attn_ctx_kernel(q_ref, k_ref, v_ref, att_ref, o_ref):
    del att_ref
    k = k_ref[...]
    v = v_ref[...]
    for h in range(ATT_GROUP):
        hs = slice(h * HEAD_DIM, (h + 1) * HEAD_DIM)
        s = lax.dot_general(q_ref[:, hs], k, NT_DIMS, preferred_element_type=F32)
        pexp = jnp.exp2(s - jnp.max(s, axis=1, keepdims=True)).astype(BF16)
        acc = jnp.dot(pexp, v, preferred_element_type=F32)
        o_ref[:, hs] = (acc[:, :HEAD_DIM] / acc[:, HEAD_DIM:]).astype(o_ref.dtype)


def _attn_ctx_call(qn, kn, vext, att, *, batch, t_len, nctx):
    nblk = t_len // nctx
    width = ATT_GROUP * HEAD_DIM
    return pl.pallas_call(
        _attn_ctx_kernel,
        grid=(batch, ATT_KV_HEADS),
        in_specs=[pl.BlockSpec((nctx, width), lambda b, g: (b * nblk, g)),
                  pl.BlockSpec((nctx, HEAD_DIM), lambda b, g: (b * nblk, g)),
                  pl.BlockSpec((nctx, 2 * HEAD_DIM), lambda b, g: (b * nblk, g)),
                  pl.BlockSpec(memory_space=pl.ANY)],
        out_specs=pl.BlockSpec((nctx, width), lambda b, g: (b * nblk, g)),
        out_shape=jax.ShapeDtypeStruct(att.shape, att.dtype),
        input_output_aliases={3: 0},
        compiler_params=_cparams(("parallel", "parallel")),
        name="ctx_attn",
    )(qn, kn, vext, att)


def _halo_valid(blk, nblk):
    prev_ok = blk >= 2
    next_ok = jnp.logical_and(blk != 0, blk != nblk - 1)
    return prev_ok, next_ok


def _shift_matrices(tb, offsets):
    r = jnp.arange(tb)[:, None]
    c = jnp.arange(tb)[None, :]
    return jnp.stack([(c == r + off) for off in offsets]).astype(BF16)


def _lru_kernel(x_ref, xp_ref, xn_ref, sh_ref, cw_ref, cb_ref, wg_ref, bg_ref, lam_ref, h_ref,
                carry_ref, a_ref, u_ref, hs_ref, hl_ref, ac_ref, *, reverse, tb, nblk):
    t = pl.program_id(1)
    blk = _time_block_index(t, nblk, reverse)

    @pl.when(t == 0)
    def _():
        carry_ref[...] = jnp.zeros_like(carry_ref)

    prev_ok, next_ok = _halo_valid(blk, nblk)
    xb = x_ref[...]
    x = xb.astype(F32)
    prev = jnp.where(prev_ok, xp_ref[...].astype(F32), 0.0)
    nxt = jnp.where(next_ok, xn_ref[...].astype(F32), 0.0)
    xm2 = jnp.dot(sh_ref[0], xb, preferred_element_type=F32)
    xm1 = jnp.dot(sh_ref[1], xb, preferred_element_type=F32)
    xp1 = jnp.dot(sh_ref[2], xb, preferred_element_type=F32)
    xc = cw_ref[0:1] * xm2 + cw_ref[1:2] * xm1 + cw_ref[2:3] * x + cw_ref[3:4] * xp1 + cb_ref[...]
    r8 = lax.broadcasted_iota(jnp.int32, (SUBLANES, 1), 0)
    p6 = prev[SUBLANES - 2:SUBLANES - 1]
    p7 = prev[SUBLANES - 1:SUBLANES]
    fix_first = jnp.where(r8 == 0, cw_ref[0:1] * p6 + cw_ref[1:2] * p7, jnp.where(r8 == 1, cw_ref[0:1] * p7, 0.0))
    fix_last = jnp.where(r8 == SUBLANES - 1, cw_ref[3:4] * nxt[0:1], 0.0)
    xc = jnp.concatenate([xc[:SUBLANES] + fix_first, xc[SUBLANES:tb - SUBLANES], xc[tb - SUBLANES:] + fix_last],
                         axis=0)

    xcb = xc.astype(BF16)
    zs = []
    for gi in range(2):
        parts = [jnp.dot(xcb[:, hb * LRU_BW:(hb + 1) * LRU_BW], wg_ref[0, gi, hb], preferred_element_type=F32)
                 for hb in range(LRU_BLOCKS)]
        zs.append(jnp.concatenate(parts, axis=-1) + bg_ref[0, gi:gi + 1, :])
    gate_r = _sigmoid(zs[0])
    gate_i = _sigmoid(zs[1])
    neg_lam = -lam_ref[0]
    softplus = jnp.maximum(neg_lam, 0.0) + jnp.log(1.0 + jnp.exp(-jnp.abs(neg_lam)))
    log_a = (-LRU_C) * gate_r * softplus
    a = jnp.exp(log_a)
    one_m_a2 = jnp.tanh(-log_a) * (1.0 + a * a)
    root = jnp.where(one_m_a2 > 0.0, one_m_a2 * lax.rsqrt(one_m_a2), 0.0)
    u = root * (gate_i * xc)

    ncol = LRU_WIDTH // LANES
    for cb in range(ncol):
        a_ref[cb] = a[:, cb * LANES:(cb + 1) * LANES]
        u_ref[cb] = u[:, cb * LANES:(cb + 1) * LANES]
    seg = LRU_SEG
    grp_rows = SUBLANES * seg
    ngrp = tb // grp_rows

    def tile_rows(grp, j):
        return pl.ds(grp * grp_rows + j, SUBLANES, stride=seg)

    def strided(ref, grp, j):
        return jnp.concatenate([ref[cb, tile_rows(grp, j), :] for cb in range(ncol)], axis=1)

    steps = range(seg - 1, -1, -1) if reverse else range(seg)
    summaries = []
    for grp in range(ngrp):
        h_loc = jnp.zeros((SUBLANES, LRU_WIDTH), F32)
        a_cum = jnp.ones((SUBLANES, LRU_WIDTH), F32)
        for j in steps:
            a_j = strided(a_ref, grp, j)
            u_j = strided(u_ref, grp, j)
            h_loc = a_j * h_loc + u_j
            a_cum = a_j * a_cum
            rows = slice((grp * seg + j) * SUBLANES, (grp * seg + j + 1) * SUBLANES)
            hl_ref[rows, :] = h_loc
            ac_ref[rows, :] = a_cum
        e, pr = h_loc, a_cum
        for sft in (1, 2, 4):
            if reverse:
                e_sh, p_sh = pltpu.roll(e, SUBLANES - sft, 0), pltpu.roll(pr, SUBLANES - sft, 0)
                valid = r8 < SUBLANES - sft
            else:
                e_sh, p_sh, valid = pltpu.roll(e, sft, 0), pltpu.roll(pr, sft, 0), r8 >= sft
            e = jnp.where(valid, e + pr * e_sh, e)
            pr = jnp.where(valid, pr * p_sh, pr)
        summaries.append((e, pr))

    c0 = carry_ref[...]
    c_in = [None] * ngrp
    for grp in (range(ngrp - 1, -1, -1) if reverse else range(ngrp)):
        e, pr = summaries[grp]
        g = e + pr * c0
        if reverse:
            c_in[grp] = jnp.where(r8 == SUBLANES - 1, c0, pltpu.roll(g, SUBLANES - 1, 0))
            c0 = jnp.broadcast_to(g[0:1], (SUBLANES, LRU_WIDTH))
        else:
            c_in[grp] = jnp.where(r8 == 0, c0, pltpu.roll(g, 1, 0))
            c0 = jnp.broadcast_to(g[SUBLANES - 1:SUBLANES], (SUBLANES, LRU_WIDTH))
    carry_ref[...] = c0

    for grp in range(ngrp):
        for j in range(seg):
            rows = slice((grp * seg + j) * SUBLANES, (grp * seg + j + 1) * SUBLANES)
            h_j = hl_ref[rows, :] + ac_ref[rows, :] * c_in[grp]
            for cb in range(ncol):
                hs_ref[cb, tile_rows(grp, j), :] = h_j[:, cb * LANES:(cb + 1) * LANES]
    for cb in range(ncol):
        h_ref[:, cb * LANES:(cb + 1) * LANES] = hs_ref[cb].astype(h_ref.dtype)


def _lru_call(p, cw, cb, wg, bg, lam, *, reverse, batch, t_len):
    m = p.shape[0]
    tb = TIME_BLOCK
    nblk = t_len // tb
    per = tb // SUBLANES
    nrow8 = m // SUBLANES
    d = 1 if reverse else 0

    def rowblk(b, t):
        return b * nblk + _time_block_index(t, nblk, reverse)

    kern = functools.partial(_lru_kernel, reverse=reverse, tb=tb, nblk=nblk)
    return pl.pallas_call(
        kern,
        grid=(batch, nblk),
        in_specs=[pl.BlockSpec((tb, LRU_WIDTH), lambda b, t: (rowblk(b, t), COL_LX // LRU_WIDTH)),
                  pl.BlockSpec((SUBLANES, LRU_WIDTH),
                               lambda b, t: (jnp.maximum(rowblk(b, t) * per - 1, 0), COL_LX // LRU_WIDTH)),
                  pl.BlockSpec((SUBLANES, LRU_WIDTH),
                               lambda b, t: (jnp.minimum((rowblk(b, t) + 1) * per, nrow8 - 1), COL_LX // LRU_WIDTH)),
                  pl.BlockSpec((LRU_CONV - 1, tb, tb), lambda b, t: (0, 0, 0)),
                  pl.BlockSpec((LRU_CONV, LRU_WIDTH), lambda b, t: (0, 0)),
                  pl.BlockSpec((1, LRU_WIDTH), lambda b, t: (0, 0)),
                  pl.BlockSpec((1, 2, LRU_BLOCKS, LRU_BW, LRU_BW), lambda b, t: (d, 0, 0, 0, 0)),
                  pl.BlockSpec((1, 2, LRU_WIDTH), lambda b, t: (d, 0, 0)),
                  pl.BlockSpec((1, 1, LRU_WIDTH), lambda b, t: (d, 0, 0))],
        out_specs=pl.BlockSpec((tb, LRU_WIDTH), lambda b, t: (rowblk(b, t), 0)),
        out_shape=jax.ShapeDtypeStruct((m, LRU_WIDTH), BF16),
        scratch_shapes=([pltpu.VMEM((SUBLANES, LRU_WIDTH), F32)]
                        + [pltpu.VMEM((LRU_WIDTH // LANES, tb, LANES), F32)] * 3
                        + [pltpu.VMEM((tb, LRU_WIDTH), F32)] * 2),
        compiler_params=_cparams(("parallel", "arbitrary")),
        name="lru_bwd" if reverse else "lru_fwd",
    )(p, p, p, _shift_matrices(tb, (-2, -1, 1)), cw, cb, wg, bg, lam)


def _select_mod(mod_ref, idx, i, tm, tiles_per_batch, nctx):
    row = (i % tiles_per_batch) * tm + lax.broadcasted_iota(jnp.int32, (tm, 1), 0)
    return jnp.where(row < nctx, mod_ref[0, idx:idx + 1, :], mod_ref[0, 6 + idx:7 + idx, :])


def _merge_kernel(of_ref, ob_ref, r_ref, att_ref, hf_ref, hb_ref, y_ref, gate_ref, x_ref, mod_ref,
                  gn_ref, gp_ref, wb_ref, wo_ref, o_ref, *, tm, tiles_per_batch, nctx):
    i = pl.program_id(0)
    o = of_ref[...].astype(F32) + ob_ref[...].astype(F32)
    gla_parts = []
    for h in range(GLA_HEADS):
        vs = slice(h * GLA_DV, (h + 1) * GLA_DV)
        gla_parts.append(_rms(o[:, vs], gn_ref[:, vs]))
    gla = jnp.concatenate(gla_parts, axis=-1) * r_ref[...].astype(F32)
    lru = (hf_ref[...].astype(F32) + hb_ref[...].astype(F32)) * y_ref[...].astype(F32)
    branches = (gla.astype(BF16), att_ref[...], lru.astype(BF16))
    mixed = None
    for bi in range(N_BRANCHES):
        gate = gate_ref[:, bi * D_MODEL:(bi + 1) * D_MODEL].astype(F32)
        term = gate * jnp.dot(branches[bi], wb_ref[bi], preferred_element_type=F32)
        mixed = term if mixed is None else mixed + term
    y = jnp.dot(mixed.astype(BF16), wo_ref[...], preferred_element_type=F32)
    gate_vec = _select_mod(mod_ref, 2, i, tm, tiles_per_batch, nctx)
    o_ref[...] = x_ref[...] + gate_vec * _rms(y, gp_ref[...])


def _merge_call(o_f, o_b, p, att, h_f, h_b, x, modsel, gn, gp, wb, wo, *, tm, t_len, nctx):
    m, d = x.shape
    tiles_per_batch = t_len // tm
    kern = functools.partial(_merge_kernel, tm=tm, tiles_per_batch=tiles_per_batch, nctx=nctx)
    row = lambda i: (i, 0)
    const2 = lambda i: (0, 0)
    return pl.pallas_call(
        kern,
        grid=(m // tm,),
        in_specs=[pl.BlockSpec((tm, d), row),
                  pl.BlockSpec((tm, d), row),
                  pl.BlockSpec((tm, d), lambda i: (i, COL_GR // D_MODEL)),
                  pl.BlockSpec((tm, d), row),
                  pl.BlockSpec((tm, d), row),
                  pl.BlockSpec((tm, d), row),
                  pl.BlockSpec((tm, d), lambda i: (i, COL_LY // D_MODEL)),
                  pl.BlockSpec((tm, N_BRANCHES * d), lambda i: (i, COL_GATE // (N_BRANCHES * D_MODEL))),
                  pl.BlockSpec((tm, d), row),
                  pl.BlockSpec((1, 12, d), lambda i: (i // tiles_per_batch, 0, 0)),
                  pl.BlockSpec((1, d), const2),
                  pl.BlockSpec((1, d), const2),
                  pl.BlockSpec((N_BRANCHES, d, d), lambda i: (0, 0, 0)),
                  pl.BlockSpec((d, d), const2)],
        out_specs=pl.BlockSpec((tm, d), row),
        out_shape=jax.ShapeDtypeStruct((m, d), F32),
        compiler_params=_cparams(("parallel",)),
        name="merge_out",
    )(o_f, o_b, p, att, h_f, h_b, p, p, x, modsel, gn, gp, wb, wo)


def _ffn_down_kernel(u_ref, up_ref, un_ref, cw_ref, cb_ref, wd_ref, x_ref, mod_ref, gp_ref, o_ref,
                     *, tb, nblk):
    i = pl.program_id(0)
    blk = i % nblk
    prev_ok, next_ok = _halo_valid(blk, nblk)
    r8 = lax.broadcasted_iota(jnp.int32, (SUBLANES, 1), 0)

    def conv(cols):
        u = u_ref[:, cols].astype(F32)
        prev = jnp.where(prev_ok, up_ref[SUBLANES - 1:SUBLANES, cols].astype(F32), 0.0)
        nxt = jnp.where(next_ok, un_ref[0:1, cols].astype(F32), 0.0)
        um1 = pltpu.roll(u, 1, 0)
        up1 = pltpu.roll(u, tb - 1, 0)
        um1 = jnp.concatenate([jnp.where(r8 == 0, prev, um1[:SUBLANES]), um1[SUBLANES:]], axis=0)
        up1 = jnp.concatenate([up1[:tb - SUBLANES], jnp.where(r8 == SUBLANES - 1, nxt, up1[tb - SUBLANES:])], axis=0)
        return cw_ref[0:1, cols] * um1 + cw_ref[1:2, cols] * u + cw_ref[2:3, cols] * up1 + cb_ref[:, cols]

    bounds = list(range(0, D_FF, FFN_COL_CHUNK)) + [D_FF]
    y = None
    for lo, hi in zip(bounds[:-1], bounds[1:]):
        act = (_silu(conv(slice(lo, hi))) * conv(slice(D_FF + lo, D_FF + hi))).astype(BF16)
        part = jnp.dot(act, wd_ref[lo:hi, :], preferred_element_type=F32)
        y = part if y is None else y + part
    gate_vec = _select_mod(mod_ref, 5, i, tb, nblk, tb)
    o_ref[...] = x_ref[...] + gate_vec * _rms(y, gp_ref[...])


def _ffn_down_call(u, cw, cb, wd, x, modsel, gp, *, t_len, latent_only):
    m, d = x.shape
    tb = TIME_BLOCK
    nblk = t_len // tb
    per = tb // SUBLANES
    nrow8 = m // SUBLANES
    kern = functools.partial(_ffn_down_kernel, tb=tb, nblk=nblk)
    if latent_only:
        out_rows = m // nblk * (nblk - 1)
        out_map = lambda i: ((i // nblk) * (nblk - 1) + jnp.maximum(i % nblk - 1, 0), 0)
    else:
        out_rows = m
        out_map = lambda i: (i, 0)
    return pl.pallas_call(
        kern,
        grid=(m // tb,),
        in_specs=[pl.BlockSpec((tb, 2 * D_FF), lambda i: (i, 0)),
                  pl.BlockSpec((SUBLANES, 2 * D_FF), lambda i: (jnp.maximum(i * per - 1, 0), 0)),
                  pl.BlockSpec((SUBLANES, 2 * D_FF), lambda i: (jnp.minimum((i + 1) * per, nrow8 - 1), 0)),
                  pl.BlockSpec((FFN_CONV, 2 * D_FF), lambda i: (0, 0)),
                  pl.BlockSpec((1, 2 * D_FF), lambda i: (0, 0)),
                  pl.BlockSpec((D_FF, d), lambda i: (0, 0)),
                  pl.BlockSpec((tb, d), lambda i: (i, 0)),
                  pl.BlockSpec((1, 12, d), lambda i: (i // nblk, 0, 0)),
                  pl.BlockSpec((1, d), lambda i: (0, 0))],
        out_specs=pl.BlockSpec((tb, d), out_map),
        out_shape=jax.ShapeDtypeStruct((out_rows, d), F32),
        compiler_params=_cparams(("arbitrary",)),
        name="ffn_down",
    )(u, u, u, cw, cb, wd, x, modsel, gp)


def _rope_tables(seq, nctx):
    rows = seq // GRID_W
    row = np.repeat(np.arange(rows), GRID_W)
    col = np.tile(np.arange(GRID_W), rows)
    axis_dim = HEAD_DIM // 2
    inv_freq = (1.0 / (np.float32(ROPE_THETA) ** (np.arange(0, axis_dim, 2, dtype=np.float32) / np.float32(axis_dim))))
    ang = np.stack([row, col], axis=-1).astype(np.float32)[:, :, None] * inv_freq.astype(np.float32)
    cos, sin = np.cos(ang), np.sin(ang)
    cos_t = np.concatenate([cos[:, 0], cos[:, 0], cos[:, 1], cos[:, 1]], axis=-1)
    sin_t = np.concatenate([-sin[:, 0], sin[:, 0], -sin[:, 1], sin[:, 1]], axis=-1)
    cos_t = np.concatenate([np.ones((nctx, HEAD_DIM), np.float32), cos_t], axis=0)
    sin_t = np.concatenate([np.zeros((nctx, HEAD_DIM), np.float32), sin_t], axis=0)
    return jnp.asarray(cos_t, F32), jnp.asarray(sin_t, F32)


def _permute_w_in(w):
    offs = [0]
    for s in IN_SPLITS:
        offs.append(offs[-1] + s)
    seg = lambda i: w[:, offs[i]:offs[i + 1]]
    gq, gk, gv, gr, lrf, lrb, aq, ak, av, lx, ly, gates = (seg(i) for i in range(12))
    pad = jnp.zeros((w.shape[0], D_IN_PAD - COL_LR - 2 * GLA_RANK), w.dtype)
    return jnp.concatenate([gq, gk, gv, gr, aq, lx, ly, gates, ak, av, lrf, lrb, pad], axis=1).astype(BF16)


def _largest_tile(t_len, cap):
    best = SUBLANES
    for cand in range(SUBLANES, cap + 1, SUBLANES):
        if t_len % cand == 0:
            best = cand
    return best


def kernel(x, c, ctx, c_ctx, w_ada, b_ada, norm_gains, w_in, gla_w_decay, gla_b_decay, gla_norm_g,
           att_q_norm_g, att_k_norm_g, lru_conv_w, lru_conv_b, lru_w_gates, lru_b_gates, lru_lambda,
           w_branch, w_out, ffn_w_up, ffn_conv_w, ffn_conv_b, ffn_w_down):
    batch, seq, d = x.shape
    nctx = ctx.shape[1]
    depth = w_ada.shape[0]
    assert d == D_MODEL and nctx == TIME_BLOCK and seq % TIME_BLOCK == 0 and seq % GRID_W == 0
    t_len = nctx + seq
    m = batch * t_len
    tm_big = _largest_tile(t_len, 1408)
    tq = _largest_tile(t_len, 768)
    assert tm_big % TIME_BLOCK == 0 or tm_big % LANES == 0
    assert tq % TIME_BLOCK == 0

    xc = jnp.concatenate([ctx, x], axis=1).reshape(m, d)

    rows = -(-(batch + 1) // SUBLANES) * SUBLANES
    cvec = jnp.zeros((rows, d), F32).at[:batch].set(c).at[batch].set(c_ctx)
    mod = _ada_call(cvec, w_ada, b_ada)
    mod = mod.reshape(depth, rows, 6, d)
    cos_t, sin_t = _rope_tables(seq, nctx)

    for l in range(depth):
        mod_c = jnp.broadcast_to(mod[l, batch][None], (batch, 6, d))
        modsel = jnp.concatenate([mod_c, mod[l, :batch]], axis=1)
        g_pre_mix, g_post_mix, g_pre_ffn, g_post_ffn = (norm_gains[l, i].reshape(1, d) for i in range(4))

        tn_in = 1024
        in_epilogues = ((_silu, COL_GR // tn_in, COL_AQ // tn_in),
                        (_gelu_tanh, COL_LY // tn_in, COL_GATE // tn_in),
                        (_sigmoid, COL_GATE // tn_in, COL_AK // tn_in))
        p = _nmm_call(xc, g_pre_mix, modsel, _permute_w_in(w_in[l]), shift_idx=0, scale_idx=1,
                      tm=tm_big, tn=tn_in, t_len=t_len, nctx=nctx, name="in_proj", epilogues=in_epilogues)

        wd_pad = jnp.zeros((2, LANES, GLA_QK), F32)
        wd_pad = wd_pad.at[0, :GLA_RANK].set(gla_w_decay[l, 0]).at[1, GLA_RANK:2 * GLA_RANK].set(gla_w_decay[l, 1])
        wd_pad = jnp.stack(_split_bf16(wd_pad), axis=1)
        bd = gla_b_decay[l].reshape(2, 1, GLA_QK)
        o_f, o_b = _gla_call(p, wd_pad, bd, batch=batch, t_len=t_len)

        qn, kn, vext = _qkprep_call(p, cos_t, sin_t, att_q_norm_g[l].reshape(1, HEAD_DIM),
                                    att_k_norm_g[l].reshape(1, HEAD_DIM), t_len=t_len)
        att = _attn_call(qn, kn, vext, batch=batch, t_len=t_len, tq=tq)
        if l < depth - 1:
            att = _attn_ctx_call(qn, kn, vext, att, batch=batch, t_len=t_len, nctx=nctx)

        wg = lru_w_gates[l].astype(BF16)
        lam = lru_lambda[l].reshape(2, 1, LRU_WIDTH)
        cb = lru_conv_b[l].reshape(1, LRU_WIDTH)
        h_f = _lru_call(p, lru_conv_w[l], cb, wg, lru_b_gates[l], lam, reverse=False, batch=batch, t_len=t_len)
        h_b = _lru_call(p, lru_conv_w[l], cb, wg, lru_b_gates[l], lam, reverse=True, batch=batch, t_len=t_len)

        xc = _merge_call(o_f, o_b, p, att, h_f, h_b, xc, modsel, gla_norm_g[l].reshape(1, GLA_V), g_post_mix,
                         w_branch[l].astype(BF16), w_out[l].astype(BF16), tm=_largest_tile(t_len, 384),
                         t_len=t_len, nctx=nctx)

        u = _nmm_call(xc, g_pre_ffn, modsel, ffn_w_up[l].astype(BF16), shift_idx=3, scale_idx=4,
                      tm=_largest_tile(t_len, 704), tn=D_FF, t_len=t_len, nctx=nctx, name="ffn_up")
        xc = _ffn_down_call(u, ffn_conv_w[l], ffn_conv_b[l].reshape(1, 2 * D_FF), ffn_w_down[l].astype(BF16),
                            xc, modsel, g_post_ffn, t_len=t_len, latent_only=(l == depth - 1))

    return xc.reshape(batch, seq, d)
```

```python
import functools

import jax
import jax.numpy as jnp
import numpy as np
from jax import lax
from jax.experimental import pallas as pl
from jax.experimental.pallas import tpu as pltpu

F32 = jnp.float32
BF16 = jnp.bfloat16

D_MODEL = 1024
NORM_EPS = 1e-6
N_BRANCHES = 3
GRID_W = 64

GLA_HEADS = 4
GLA_DK = D_MODEL // (2 * GLA_HEADS)
GLA_DV = D_MODEL // GLA_HEADS
GLA_QK = GLA_HEADS * GLA_DK
GLA_V = GLA_HEADS * GLA_DV
GLA_RANK = 16
GLA_TEMP = 16.0
GLA_CHUNK = 64
GLA_PHASE_LAG = 3

HEAD_DIM = 128
ATT_Q_HEADS = D_MODEL // HEAD_DIM
ATT_KV_HEADS = ATT_Q_HEADS // 4
ATT_GROUP = ATT_Q_HEADS // ATT_KV_HEADS
ATT_Q = ATT_Q_HEADS * HEAD_DIM
ATT_KV = ATT_KV_HEADS * HEAD_DIM
ROPE_THETA = 10000.0
LOG2_E = 1.4426950408889634

LRU_WIDTH = D_MODEL
LRU_BLOCKS = 8
LRU_BW = LRU_WIDTH // LRU_BLOCKS
LRU_C = 8.0
LRU_CONV = 4
LRU_CONV_LEFT = 2
LRU_SEG = 4

D_FF = 2816
FFN_CONV = 3
FFN_COL_CHUNK = 256

IN_SPLITS = (GLA_QK, GLA_QK, GLA_V, GLA_V, GLA_RANK, GLA_RANK,
             ATT_Q, ATT_KV, ATT_KV, LRU_WIDTH, LRU_WIDTH, N_BRANCHES * D_MODEL)

SUBLANES = 8
LANES = 128
TIME_BLOCK = 256

COL_GQ = 0
COL_GK = 512
COL_GV = 1024
COL_GR = 2048
COL_AQ = 3072
COL_LX = 4096
COL_LY = 5120
COL_GATE = 6144
COL_AK = 9216
COL_AV = 9472
COL_LR = 9728
D_IN_PAD = 10240

VMEM_LIMIT = 56 * 1024 * 1024


def _cparams(sem):
    return pltpu.CompilerParams(dimension_semantics=sem, vmem_limit_bytes=VMEM_LIMIT)


def _sigmoid(x):
    return 0.5 * jnp.tanh(0.5 * x) + 0.5


def _silu(x):
    hx = 0.5 * x
    return hx * jnp.tanh(hx) + hx


def _gelu_tanh(x):
    hx = 0.5 * x
    inner = x * (0.7978845608028654 + (0.7978845608028654 * 0.044715) * (x * x))
    return hx * jnp.tanh(inner) + hx


def _split_bf16(x):
    hi = x.astype(BF16)
    lo = (x - hi.astype(F32)).astype(BF16)
    return hi, lo


def _rms(x, g):
    ms = jnp.mean(x * x, axis=-1, keepdims=True)
    return x * lax.rsqrt(ms + NORM_EPS) * g


def _ada_kernel(c_ref, w_ref, b_ref, o_ref):
    c = c_ref[...]
    o_ref[0] = jnp.dot(_silu(c), w_ref[0], preferred_element_type=F32,
                       precision=lax.Precision.HIGHEST) + b_ref[0]


def _ada_call(cvec, w_ada, b_ada):
    depth, d, n = w_ada.shape
    tn = 1536
    rows = cvec.shape[0]
    return pl.pallas_call(
        _ada_kernel,
        grid=(depth, n // tn),
        in_specs=[pl.BlockSpec((rows, d), lambda l, j: (0, 0)),
                  pl.BlockSpec((1, d, tn), lambda l, j: (l, 0, j)),
                  pl.BlockSpec((1, 1, tn), lambda l, j: (l, 0, j))],
        out_specs=pl.BlockSpec((1, rows, tn), lambda l, j: (l, 0, j)),
        out_shape=jax.ShapeDtypeStruct((depth, rows, n), F32),
        compiler_params=_cparams(("parallel", "parallel")),
        name="ada_mod",
    )(cvec, w_ada, b_ada.reshape(depth, 1, n))


def _nmm_kernel(x_ref, g_ref, mod_ref, w_ref, o_ref, h_ref, *, shift_idx, scale_idx, tm, nctx, tiles_per_batch):
    i = pl.program_id(0)

    @pl.when(pl.program_id(1) == 0)
    def _():
        y = _rms(x_ref[...], g_ref[...])
        row = (i % tiles_per_batch) * tm + lax.broadcasted_iota(jnp.int32, (tm, 1), 0)
        is_ctx = row < nctx
        shift = jnp.where(is_ctx, mod_ref[0, shift_idx:shift_idx + 1, :], mod_ref[0, 6 + shift_idx:7 + shift_idx, :])
        scale = jnp.where(is_ctx, mod_ref[0, scale_idx:scale_idx + 1, :], mod_ref[0, 6 + scale_idx:7 + scale_idx, :])
        h_ref[...] = (y * (1.0 + scale) + shift).astype(BF16)

    o_ref[...] = jnp.dot(h_ref[...], w_ref[...], preferred_element_type=F32).astype(o_ref.dtype)


def _nmm_call(x, g, modsel, w, *, shift_idx, scale_idx, tm, tn, t_len, nctx, name):
    m, d = x.shape
    n = w.shape[1]
    tiles_per_batch = t_len // tm
    kern = functools.partial(_nmm_kernel, shift_idx=shift_idx, scale_idx=scale_idx, tm=tm, nctx=nctx,
                             tiles_per_batch=tiles_per_batch)
    return pl.pallas_call(
        kern,
        grid=(m // tm, n // tn),
        in_specs=[pl.BlockSpec((tm, d), lambda i, j: (i, 0)),
                  pl.BlockSpec((1, d), lambda i, j: (0, 0)),
                  pl.BlockSpec((1, 12, d), lambda i, j: (i // tiles_per_batch, 0, 0)),
                  pl.BlockSpec((d, tn), lambda i, j: (0, j))],
        out_specs=pl.BlockSpec((tm, tn), lambda i, j: (i, j)),
        out_shape=jax.ShapeDtypeStruct((m, n), BF16),
        scratch_shapes=[pltpu.VMEM((tm, d), BF16)],
        compiler_params=_cparams(("parallel", "arbitrary")),
        name=name,
    )(x, g, modsel, w)


def _log_sigmoid(z):
    return jnp.minimum(z, 0.0) - jnp.log(1.0 + jnp.exp(-jnp.abs(z)))


def _gla_kernel(qf_ref, kf_ref, vf_ref, lrf_ref, qb_ref, kb_ref, vb_ref, lrb_ref, wd_ref, bd_ref,
                of_ref, ob_ref, stf_ref, stb_ref, *, tb):
    @pl.when(pl.program_id(1) == 0)
    def _():
        stf_ref[...] = jnp.zeros_like(stf_ref)
        stb_ref[...] = jnp.zeros_like(stb_ref)

    fwd = _gla_direction(qf_ref, kf_ref, vf_ref, lrf_ref, wd_ref, bd_ref, of_ref, stf_ref, reverse=False, tb=tb)
    bwd = _gla_direction(qb_ref, kb_ref, vb_ref, lrb_ref, wd_ref, bd_ref, ob_ref, stb_ref, reverse=True, tb=tb)
    done = object()
    running = True
    for _ in range(GLA_PHASE_LAG):
        running = next(fwd, done) is not done
    while running:
        step_f = next(fwd, done)
        step_b = next(bwd, done)
        running = step_f is not done or step_b is not done


def _gla_direction(q_ref, k_ref, v_ref, lr_ref, wd_ref, bd_ref, o_ref, st_ref, *, reverse, tb):
    c = GLA_CHUNK
    nchunk = tb // c
    d = 1 if reverse else 0

    lr = lr_ref[...]
    z = (jnp.dot(lr, wd_ref[d, 0], preferred_element_type=F32)
         + jnp.dot(lr, wd_ref[d, 1], preferred_element_type=F32) + bd_ref[d])
    yield
    log_a = _log_sigmoid(z) * (1.0 / GLA_TEMP)
    r = lax.broadcasted_iota(jnp.int32, (tb, tb), 0)
    s = lax.broadcasted_iota(jnp.int32, (tb, tb), 1)
    keep = ((r // c) == (s // c)) & ((s >= r) if reverse else (s <= r))
    tri = jnp.where(keep, 1.0, 0.0).astype(BF16)
    la_hi, la_lo = _split_bf16(log_a)
    cum = jnp.dot(tri, la_hi, preferred_element_type=F32) + jnp.dot(tri, la_lo, preferred_element_type=F32)
    yield

    last = [(ci * c) if reverse else (ci * c + c - 1) for ci in range(nchunk)]
    tots = [cum[i:i + 1] for i in last]
    tot_rows = jnp.concatenate([jnp.broadcast_to(tt, (c, GLA_QK)) for tt in tots], axis=0)
    e_cum = jnp.exp(cum)
    e_inv = jnp.exp(-cum)
    e_end = jnp.exp(tot_rows - cum)
    e_tot = [jnp.exp(tt) for tt in tots]

    order = range(nchunk - 1, -1, -1) if reverse else range(nchunk)
    for h in range(GLA_HEADS):
        yield
        hs = slice(h * GLA_DK, (h + 1) * GLA_DK)
        vs = slice(h * GLA_DV, (h + 1) * GLA_DV)
        q = q_ref[:, hs].astype(F32) * (GLA_DK ** -0.5)
        k = k_ref[:, hs].astype(F32)
        v = v_ref[:, vs]
        qd = (q * e_cum[:, hs]).astype(BF16)
        ki = (k * e_inv[:, hs]).astype(BF16)
        ke = (k * e_end[:, hs]).astype(BF16)
        att = lax.dot_general(qd, ki, NT_DIMS, preferred_element_type=F32)
        o_intra = jnp.dot(jnp.where(keep, att, 0.0).astype(BF16), v, preferred_element_type=F32)
        st = st_ref[h]
        o_inter = [None] * nchunk
        for ci in order:
            rows = slice(ci * c, (ci + 1) * c)
            o_inter[ci] = lax.dot_general(qd[rows], st.astype(BF16), NT_DIMS, preferred_element_type=F32)
            upd = lax.dot_general(v[rows], ke[rows], (((0,), (0,)), ((), ())), preferred_element_type=F32)
            st = st * e_tot[ci][:, hs] + upd
        st_ref[h] = st
        o_ref[:, vs] = (o_intra + jnp.concatenate(o_inter, axis=0)).astype(o_ref.dtype)


def _time_block_index(t, nblk, reverse):
    if not reverse:
        return t
    return jnp.where(t == 0, 0, nblk - t)


def _gla_call(p, wd_pad, bd, *, batch, t_len):
    m = p.shape[0]
    tb = TIME_BLOCK
    nblk = t_len // tb

    def blocks(reverse):
        def rowblk(b, t):
            return b * nblk + _time_block_index(t, nblk, reverse)
        return [pl.BlockSpec((tb, GLA_QK), lambda b, t: (rowblk(b, t), COL_GQ // GLA_QK)),
                pl.BlockSpec((tb, GLA_QK), lambda b, t: (rowblk(b, t), COL_GK // GLA_QK)),
                pl.BlockSpec((tb, GLA_V), lambda b, t: (rowblk(b, t), COL_GV // GLA_V)),
                pl.BlockSpec((tb, LANES), lambda b, t: (rowblk(b, t), COL_LR // LANES))], \
            pl.BlockSpec((tb, GLA_V), lambda b, t: (rowblk(b, t), 0))

    in_f, out_f = blocks(False)
    in_b, out_b = blocks(True)
    state = pltpu.VMEM((GLA_HEADS, GLA_DV, GLA_DK), F32)
    return pl.pallas_call(
        functools.partial(_gla_kernel, tb=tb),
        grid=(batch, nblk),
        in_specs=in_f + in_b + [pl.BlockSpec((2, 2, LANES, GLA_QK), lambda b, t: (0, 0, 0, 0)),
                                pl.BlockSpec((2, 1, GLA_QK), lambda b, t: (0, 0, 0))],
        out_specs=[out_f, out_b],
        out_shape=[jax.ShapeDtypeStruct((m, GLA_V), BF16)] * 2,
        scratch_shapes=[state, state],
        compiler_params=_cparams(("parallel", "arbitrary")),
        name="gla_scan",
    )(p, p, p, p, p, p, p, p, wd_pad, bd)


def _qkprep_kernel(q_ref, k_ref, v_ref, cos_ref, sin_ref, gq_ref, gk_ref, qo_ref, ko_ref, vo_ref):
    cos = cos_ref[...]
    sin = sin_ref[...]
    lane = lax.broadcasted_iota(jnp.int32, (1, HEAD_DIM), 1)
    first = (lane % (HEAD_DIM // 2)) < (HEAD_DIM // 4)

    def prep(x, g, scale):
        y = _rms(x.astype(F32), g)
        partner = jnp.where(first, pltpu.roll(y, HEAD_DIM - HEAD_DIM // 4, 1), pltpu.roll(y, HEAD_DIM // 4, 1))
        out = y * cos + partner * sin
        if scale != 1.0:
            out = out * scale
        return out.astype(BF16)

    for h in range(ATT_Q_HEADS):
        hs = slice(h * HEAD_DIM, (h + 1) * HEAD_DIM)
        qo_ref[:, hs] = prep(q_ref[:, hs], gq_ref[...], HEAD_DIM ** -0.5 * LOG2_E)
    for h in range(ATT_KV_HEADS):
        hs = slice(h * HEAD_DIM, (h + 1) * HEAD_DIM)
        ko_ref[:, hs] = prep(k_ref[:, hs], gk_ref[...], 1.0)
        vo_ref[:, 2 * h * HEAD_DIM:(2 * h + 1) * HEAD_DIM] = v_ref[:, hs]
        vo_ref[:, (2 * h + 1) * HEAD_DIM:(2 * h + 2) * HEAD_DIM] = jnp.ones((v_ref.shape[0], HEAD_DIM), BF16)


def _qkprep_call(p, cos_t, sin_t, gq, gk, *, t_len):
    m = p.shape[0]
    tb = TIME_BLOCK
    nblk = t_len // tb
    return pl.pallas_call(
        _qkprep_kernel,
        grid=(m // tb,),
        in_specs=[pl.BlockSpec((tb, ATT_Q), lambda i: (i, COL_AQ // ATT_Q)),
                  pl.BlockSpec((tb, ATT_KV), lambda i: (i, COL_AK // ATT_KV)),
                  pl.BlockSpec((tb, ATT_KV), lambda i: (i, COL_AV // ATT_KV)),
                  pl.BlockSpec((tb, HEAD_DIM), lambda i: (i % nblk, 0)),
                  pl.BlockSpec((tb, HEAD_DIM), lambda i: (i % nblk, 0)),
                  pl.BlockSpec((1, HEAD_DIM), lambda i: (0, 0)),
                  pl.BlockSpec((1, HEAD_DIM), lambda i: (0, 0))],
        out_specs=[pl.BlockSpec((tb, ATT_Q), lambda i: (i, 0)),
                   pl.BlockSpec((tb, ATT_KV), lambda i: (i, 0)),
                   pl.BlockSpec((tb, 2 * ATT_KV), lambda i: (i, 0))],
        out_shape=[jax.ShapeDtypeStruct((m, ATT_Q), BF16), jax.ShapeDtypeStruct((m, ATT_KV), BF16),
                   jax.ShapeDtypeStruct((m, 2 * ATT_KV), BF16)],
        compiler_params=_cparams(("parallel",)),
        name="qk_prep",
    )(p, p, p, cos_t, sin_t, gq, gk)


NEG_BIG = -1e30


ATT_ROW_SUB = 64
ATT_HEADS_PER_STEP = 2
NT_DIMS = (((1,), (1,)), ((), ()))


def _attn_kernel(q_ref, k_ref, v_ref, o_ref, s_ref, p_ref, a_ref, m_ref, acc_ref, *, tq, tk, t_len):
    nkv = t_len // tk
    sub = ATT_ROW_SUB
    nlt = tk // LANES
    nh = q_ref.shape[1] // HEAD_DIM
    m_ref[...] = jnp.full_like(m_ref, NEG_BIG)
    acc_ref[...] = jnp.zeros_like(acc_ref)
    items = [(h, j) for h in range(nh) for j in range(nkv)]

    def scores(i):
        h, j = items[i]
        q = q_ref[:, h * HEAD_DIM:(h + 1) * HEAD_DIM]
        s_ref[i % 2] = lax.dot_general(q, k_ref[j * tk:(j + 1) * tk, :], NT_DIMS, preferred_element_type=F32)

    def exponentials(i):
        h, _ = items[i]
        slot = i % 2
        for r in range(tq // sub):
            rows = slice(r * sub, (r + 1) * sub)
            tiles = [s_ref[slot, rows, t * LANES:(t + 1) * LANES] for t in range(nlt)]
            mx = tiles[0]
            for t in range(1, nlt):
                mx = jnp.maximum(mx, tiles[t])
            m_prev = m_ref[h, rows, :]
            m_new = jnp.maximum(m_prev, jnp.max(mx, axis=1, keepdims=True))
            a_ref[slot, rows, :] = jnp.exp2(m_prev - m_new)
            m_ref[h, rows, :] = m_new
            for t in range(nlt):
                p_ref[slot, rows, t * LANES:(t + 1) * LANES] = jnp.exp2(tiles[t] - m_new).astype(BF16)

    def weighted_values(i):
        h, j = items[i]
        slot = i % 2
        pv = jnp.dot(p_ref[slot], v_ref[j * tk:(j + 1) * tk, :], preferred_element_type=F32)
        alpha = a_ref[slot]
        acc_ref[h] = jnp.concatenate([alpha, alpha], axis=1) * acc_ref[h] + pv

    n = len(items)
    for step in range(n + 2):
        if step < n:
            scores(step)
        if 1 <= step <= n:
            exponentials(step - 1)
        if step >= 2:
            weighted_values(step - 2)

    for h in range(nh):
        acc = acc_ref[h]
        o_ref[:, h * HEAD_DIM:(h + 1) * HEAD_DIM] = (acc[:, :HEAD_DIM] / acc[:, HEAD_DIM:]).astype(o_ref.dtype)


def _attn_call(qn, kn, vext, *, batch, t_len, tq):
    m = qn.shape[0]
    nq = t_len // tq
    tk = tq
    assert tq % ATT_ROW_SUB == 0 and tk % LANES == 0
    kern = functools.partial(_attn_kernel, tq=tq, tk=tk, t_len=t_len)
    nh = ATT_HEADS_PER_STEP
    hsteps = ATT_GROUP // nh
    width = nh * HEAD_DIM
    return pl.pallas_call(
        kern,
        grid=(batch, ATT_KV_HEADS, nq, hsteps),
        in_specs=[pl.BlockSpec((tq, width), lambda b, g, i, h: (b * nq + i, g * hsteps + h)),
                  pl.BlockSpec((t_len, HEAD_DIM), lambda b, g, i, h: (b, g)),
                  pl.BlockSpec((t_len, 2 * HEAD_DIM), lambda b, g, i, h: (b, g))],
        out_specs=pl.BlockSpec((tq, width), lambda b, g, i, h: (b * nq + i, g * hsteps + h)),
        out_shape=jax.ShapeDtypeStruct((m, ATT_Q), BF16),
        scratch_shapes=[pltpu.VMEM((2, tq, tk), F32), pltpu.VMEM((2, tq, tk), BF16),
                        pltpu.VMEM((2, tq, LANES), F32), pltpu.VMEM((nh, tq, LANES), F32),
                        pltpu.VMEM((nh, tq, 2 * HEAD_DIM), F32)],
        compiler_params=_cparams(("parallel", "parallel", "arbitrary", "arbitrary")),
        name="flash_attn",
    )(qn, kn, vext)


def _attn_ctx_kernel(q_ref, k_ref, v_ref, att_ref, o_ref):
    del att_ref
    k = k_ref[...]
    v = v_ref[...]
    for h in range(ATT_GROUP):
        hs = slice(h * HEAD_DIM, (h + 1) * HEAD_DIM)
        s = lax.dot_general(q_ref[:, hs], k, NT_DIMS, preferred_element_type=F32)
        pexp = jnp.exp2(s - jnp.max(s, axis=1, keepdims=True)).astype(BF16)
        acc = jnp.dot(pexp, v, preferred_element_type=F32)
        o_ref[:, hs] = (acc[:, :HEAD_DIM] / acc[:, HEAD_DIM:]).astype(o_ref.dtype)


def _attn_ctx_call(qn, kn, vext, att, *, batch, t_len, nctx):
    nblk = t_len // nctx
    width = ATT_GROUP * HEAD_DIM
    return pl.pallas_call(
        _attn_ctx_kernel,
        grid=(batch, ATT_KV_HEADS),
        in_specs=[pl.BlockSpec((nctx, width), lambda b, g: (b * nblk, g)),
                  pl.BlockSpec((nctx, HEAD_DIM), lambda b, g: (b * nblk, g)),
                  pl.BlockSpec((nctx, 2 * HEAD_DIM), lambda b, g: (b * nblk, g)),
                  pl.BlockSpec(memory_space=pl.ANY)],
        out_specs=pl.BlockSpec((nctx, width), lambda b, g: (b * nblk, g)),
        out_shape=jax.ShapeDtypeStruct(att.shape, att.dtype),
        input_output_aliases={3: 0},
        compiler_params=_cparams(("parallel", "parallel")),
        name="ctx_attn",
    )(qn, kn, vext, att)


def _halo_valid(blk, nblk):
    prev_ok = blk >= 2
    next_ok = jnp.logical_and(blk != 0, blk != nblk - 1)
    return prev_ok, next_ok


def _shift_matrices(tb, offsets):
    r = jnp.arange(tb)[:, None]
    c = jnp.arange(tb)[None, :]
    return jnp.stack([(c == r + off) for off in offsets]).astype(BF16)


def _lru_kernel(x_ref, xp_ref, xn_ref, sh_ref, cw_ref, cb_ref, wg_ref, bg_ref, lam_ref, h_ref,
                carry_ref, a_ref, u_ref, hs_ref, hl_ref, ac_ref, *, reverse, tb, nblk):
    t = pl.program_id(1)
    blk = _time_block_index(t, nblk, reverse)

    @pl.when(t == 0)
    def _():
        carry_ref[...] = jnp.zeros_like(carry_ref)

    prev_ok, next_ok = _halo_valid(blk, nblk)
    xb = x_ref[...]
    x = xb.astype(F32)
    prev = jnp.where(prev_ok, xp_ref[...].astype(F32), 0.0)
    nxt = jnp.where(next_ok, xn_ref[...].astype(F32), 0.0)
    xm2 = jnp.dot(sh_ref[0], xb, preferred_element_type=F32)
    xm1 = jnp.dot(sh_ref[1], xb, preferred_element_type=F32)
    xp1 = jnp.dot(sh_ref[2], xb, preferred_element_type=F32)
    xc = cw_ref[0:1] * xm2 + cw_ref[1:2] * xm1 + cw_ref[2:3] * x + cw_ref[3:4] * xp1 + cb_ref[...]
    r8 = lax.broadcasted_iota(jnp.int32, (SUBLANES, 1), 0)
    p6 = prev[SUBLANES - 2:SUBLANES - 1]
    p7 = prev[SUBLANES - 1:SUBLANES]
    fix_first = jnp.where(r8 == 0, cw_ref[0:1] * p6 + cw_ref[1:2] * p7, jnp.where(r8 == 1, cw_ref[0:1] * p7, 0.0))
    fix_last = jnp.where(r8 == SUBLANES - 1, cw_ref[3:4] * nxt[0:1], 0.0)
    xc = jnp.concatenate([xc[:SUBLANES] + fix_first, xc[SUBLANES:tb - SUBLANES], xc[tb - SUBLANES:] + fix_last],
                         axis=0)

    xcb = xc.astype(BF16)
    zs = []
    for gi in range(2):
        parts = [jnp.dot(xcb[:, hb * LRU_BW:(hb + 1) * LRU_BW], wg_ref[0, gi, hb], preferred_element_type=F32)
                 for hb in range(LRU_BLOCKS)]
        zs.append(jnp.concatenate(parts, axis=-1) + bg_ref[0, gi:gi + 1, :])
    gate_r = _sigmoid(zs[0])
    gate_i = _sigmoid(zs[1])
    neg_lam = -lam_ref[0]
    softplus = jnp.maximum(neg_lam, 0.0) + jnp.log(1.0 + jnp.exp(-jnp.abs(neg_lam)))
    log_a = (-LRU_C) * gate_r * softplus
    a = jnp.exp(log_a)
    one_m_a2 = jnp.tanh(-log_a) * (1.0 + a * a)
    root = jnp.where(one_m_a2 > 0.0, one_m_a2 * lax.rsqrt(one_m_a2), 0.0)
    u = root * (gate_i * xc)

    ncol = LRU_WIDTH // LANES
    for cb in range(ncol):
        a_ref[cb] = a[:, cb * LANES:(cb + 1) * LANES]
        u_ref[cb] = u[:, cb * LANES:(cb + 1) * LANES]
    seg = LRU_SEG
    grp_rows = SUBLANES * seg
    ngrp = tb // grp_rows

    def tile_rows(grp, j):
        return pl.ds(grp * grp_rows + j, SUBLANES, stride=seg)

    def strided(ref, grp, j):
        return jnp.concatenate([ref[cb, tile_rows(grp, j), :] for cb in range(ncol)], axis=1)

    steps = range(seg - 1, -1, -1) if reverse else range(seg)
    summaries = []
    for grp in range(ngrp):
        h_loc = jnp.zeros((SUBLANES, LRU_WIDTH), F32)
        a_cum = jnp.ones((SUBLANES, LRU_WIDTH), F32)
        for j in steps:
            a_j = strided(a_ref, grp, j)
            u_j = strided(u_ref, grp, j)
            h_loc = a_j * h_loc + u_j
            a_cum = a_j * a_cum
            rows = slice((grp * seg + j) * SUBLANES, (grp * seg + j + 1) * SUBLANES)
            hl_ref[rows, :] = h_loc
            ac_ref[rows, :] = a_cum
        e, pr = h_loc, a_cum
        for sft in (1, 2, 4):
            if reverse:
                e_sh, p_sh = pltpu.roll(e, SUBLANES - sft, 0), pltpu.roll(pr, SUBLANES - sft, 0)
                valid = r8 < SUBLANES - sft
            else:
                e_sh, p_sh, valid = pltpu.roll(e, sft, 0), pltpu.roll(pr, sft, 0), r8 >= sft
            e = jnp.where(valid, e + pr * e_sh, e)
            pr = jnp.where(valid, pr * p_sh, pr)
        summaries.append((e, pr))

    c0 = carry_ref[...]
    c_in = [None] * ngrp
    for grp in (range(ngrp - 1, -1, -1) if reverse else range(ngrp)):
        e, pr = summaries[grp]
        g = e + pr * c0
        if reverse:
            c_in[grp] = jnp.where(r8 == SUBLANES - 1, c0, pltpu.roll(g, SUBLANES - 1, 0))
            c0 = jnp.broadcast_to(g[0:1], (SUBLANES, LRU_WIDTH))
        else:
            c_in[grp] = jnp.where(r8 == 0, c0, pltpu.roll(g, 1, 0))
            c0 = jnp.broadcast_to(g[SUBLANES - 1:SUBLANES], (SUBLANES, LRU_WIDTH))
    carry_ref[...] = c0

    for grp in range(ngrp):
        for j in range(seg):
            rows = slice((grp * seg + j) * SUBLANES, (grp * seg + j + 1) * SUBLANES)
            h_j = hl_ref[rows, :] + ac_ref[rows, :] * c_in[grp]
            for cb in range(ncol):
                hs_ref[cb, tile_rows(grp, j), :] = h_j[:, cb * LANES:(cb + 1) * LANES]
    for cb in range(ncol):
        h_ref[:, cb * LANES:(cb + 1) * LANES] = hs_ref[cb].astype(h_ref.dtype)


def _lru_call(p, cw, cb, wg, bg, lam, *, reverse, batch, t_len):
    m = p.shape[0]
    tb = TIME_BLOCK
    nblk = t_len // tb
    per = tb // SUBLANES
    nrow8 = m // SUBLANES
    d = 1 if reverse else 0

    def rowblk(b, t):
        return b * nblk + _time_block_index(t, nblk, reverse)

    kern = functools.partial(_lru_kernel, reverse=reverse, tb=tb, nblk=nblk)
    return pl.pallas_call(
        kern,
        grid=(batch, nblk),
        in_specs=[pl.BlockSpec((tb, LRU_WIDTH), lambda b, t: (rowblk(b, t), COL_LX // LRU_WIDTH)),
                  pl.BlockSpec((SUBLANES, LRU_WIDTH),
                               lambda b, t: (jnp.maximum(rowblk(b, t) * per - 1, 0), COL_LX // LRU_WIDTH)),
                  pl.BlockSpec((SUBLANES, LRU_WIDTH),
                               lambda b, t: (jnp.minimum((rowblk(b, t) + 1) * per, nrow8 - 1), COL_LX // LRU_WIDTH)),
                  pl.BlockSpec((LRU_CONV - 1, tb, tb), lambda b, t: (0, 0, 0)),
                  pl.BlockSpec((LRU_CONV, LRU_WIDTH), lambda b, t: (0, 0)),
                  pl.BlockSpec((1, LRU_WIDTH), lambda b, t: (0, 0)),
                  pl.BlockSpec((1, 2, LRU_BLOCKS, LRU_BW, LRU_BW), lambda b, t: (d, 0, 0, 0, 0)),
                  pl.BlockSpec((1, 2, LRU_WIDTH), lambda b, t: (d, 0, 0)),
                  pl.BlockSpec((1, 1, LRU_WIDTH), lambda b, t: (d, 0, 0))],
        out_specs=pl.BlockSpec((tb, LRU_WIDTH), lambda b, t: (rowblk(b, t), 0)),
        out_shape=jax.ShapeDtypeStruct((m, LRU_WIDTH), BF16),
        scratch_shapes=([pltpu.VMEM((SUBLANES, LRU_WIDTH), F32)]
                        + [pltpu.VMEM((LRU_WIDTH // LANES, tb, LANES), F32)] * 3
                        + [pltpu.VMEM((tb, LRU_WIDTH), F32)] * 2),
        compiler_params=_cparams(("parallel", "arbitrary")),
        name="lru_bwd" if reverse else "lru_fwd",
    )(p, p, p, _shift_matrices(tb, (-2, -1, 1)), cw, cb, wg, bg, lam)


def _select_mod(mod_ref, idx, i, tm, tiles_per_batch, nctx):
    row = (i % tiles_per_batch) * tm + lax.broadcasted_iota(jnp.int32, (tm, 1), 0)
    return jnp.where(row < nctx, mod_ref[0, idx:idx + 1, :], mod_ref[0, 6 + idx:7 + idx, :])


def _merge_kernel(of_ref, ob_ref, r_ref, att_ref, hf_ref, hb_ref, y_ref, gate_ref, x_ref, mod_ref,
                  gn_ref, gp_ref, wb_ref, wo_ref, o_ref, *, tm, tiles_per_batch, nctx):
    i = pl.program_id(0)
    o = of_ref[...].astype(F32) + ob_ref[...].astype(F32)
    gla_parts = []
    for h in range(GLA_HEADS):
        vs = slice(h * GLA_DV, (h + 1) * GLA_DV)
        gla_parts.append(_rms(o[:, vs], gn_ref[:, vs]))
    gla = jnp.concatenate(gla_parts, axis=-1) * _silu(r_ref[...].astype(F32))
    lru = (hf_ref[...].astype(F32) + hb_ref[...].astype(F32)) * _gelu_tanh(y_ref[...].astype(F32))
    branches = (gla.astype(BF16), att_ref[...], lru.astype(BF16))
    mixed = None
    for bi in range(N_BRANCHES):
        gate = _sigmoid(gate_ref[:, bi * D_MODEL:(bi + 1) * D_MODEL].astype(F32))
        term = gate * jnp.dot(branches[bi], wb_ref[bi], preferred_element_type=F32)
        mixed = term if mixed is None else mixed + term
    y = jnp.dot(mixed.astype(BF16), wo_ref[...], preferred_element_type=F32)
    gate_vec = _select_mod(mod_ref, 2, i, tm, tiles_per_batch, nctx)
    o_ref[...] = x_ref[...] + gate_vec * _rms(y, gp_ref[...])


def _merge_call(o_f, o_b, p, att, h_f, h_b, x, modsel, gn, gp, wb, wo, *, tm, t_len, nctx):
    m, d = x.shape
    tiles_per_batch = t_len // tm
    kern = functools.partial(_merge_kernel, tm=tm, tiles_per_batch=tiles_per_batch, nctx=nctx)
    row = lambda i: (i, 0)
    const2 = lambda i: (0, 0)
    return pl.pallas_call(
        kern,
        grid=(m // tm,),
        in_specs=[pl.BlockSpec((tm, d), row),
                  pl.BlockSpec((tm, d), row),
                  pl.BlockSpec((tm, d), lambda i: (i, COL_GR // D_MODEL)),
                  pl.BlockSpec((tm, d), row),
                  pl.BlockSpec((tm, d), row),
                  pl.BlockSpec((tm, d), row),
                  pl.BlockSpec((tm, d), lambda i: (i, COL_LY // D_MODEL)),
                  pl.BlockSpec((tm, N_BRANCHES * d), lambda i: (i, COL_GATE // (N_BRANCHES * D_MODEL))),
                  pl.BlockSpec((tm, d), row),
                  pl.BlockSpec((1, 12, d), lambda i: (i // tiles_per_batch, 0, 0)),
                  pl.BlockSpec((1, d), const2),
                  pl.BlockSpec((1, d), const2),
                  pl.BlockSpec((N_BRANCHES, d, d), lambda i: (0, 0, 0), pipeline_mode=pl.Buffered(1)),
                  pl.BlockSpec((d, d), const2, pipeline_mode=pl.Buffered(1))],
        out_specs=pl.BlockSpec((tm, d), row),
        out_shape=jax.ShapeDtypeStruct((m, d), F32),
        compiler_params=_cparams(("parallel",)),
        name="merge_out",
    )(o_f, o_b, p, att, h_f, h_b, p, p, x, modsel, gn, gp, wb, wo)


def _ffn_down_kernel(u_ref, up_ref, un_ref, cw_ref, cb_ref, wd_ref, x_ref, mod_ref, gp_ref, o_ref,
                     *, tb, nblk):
    i = pl.program_id(0)
    blk = i % nblk
    prev_ok, next_ok = _halo_valid(blk, nblk)
    r8 = lax.broadcasted_iota(jnp.int32, (SUBLANES, 1), 0)

    def conv(cols):
        u = u_ref[:, cols].astype(F32)
        prev = jnp.where(prev_ok, up_ref[SUBLANES - 1:SUBLANES, cols].astype(F32), 0.0)
        nxt = jnp.where(next_ok, un_ref[0:1, cols].astype(F32), 0.0)
        um1 = pltpu.roll(u, 1, 0)
        up1 = pltpu.roll(u, tb - 1, 0)
        um1 = jnp.concatenate([jnp.where(r8 == 0, prev, um1[:SUBLANES]), um1[SUBLANES:]], axis=0)
        up1 = jnp.concatenate([up1[:tb - SUBLANES], jnp.where(r8 == SUBLANES - 1, nxt, up1[tb - SUBLANES:])], axis=0)
        return cw_ref[0:1, cols] * um1 + cw_ref[1:2, cols] * u + cw_ref[2:3, cols] * up1 + cb_ref[:, cols]

    bounds = list(range(0, D_FF, FFN_COL_CHUNK)) + [D_FF]
    y = None
    for lo, hi in zip(bounds[:-1], bounds[1:]):
        act = (_silu(conv(slice(lo, hi))) * conv(slice(D_FF + lo, D_FF + hi))).astype(BF16)
        part = jnp.dot(act, wd_ref[lo:hi, :], preferred_element_type=F32)
        y = part if y is None else y + part
    gate_vec = _select_mod(mod_ref, 5, i, tb, nblk, tb)
    o_ref[...] = x_ref[...] + gate_vec * _rms(y, gp_ref[...])


def _ffn_down_call(u, cw, cb, wd, x, modsel, gp, *, t_len, latent_only):
    m, d = x.shape
    tb = TIME_BLOCK
    nblk = t_len // tb
    per = tb // SUBLANES
    nrow8 = m // SUBLANES
    kern = functools.partial(_ffn_down_kernel, tb=tb, nblk=nblk)
    if latent_only:
        out_rows = m // nblk * (nblk - 1)
        out_map = lambda i: ((i // nblk) * (nblk - 1) + jnp.maximum(i % nblk - 1, 0), 0)
    else:
        out_rows = m
        out_map = lambda i: (i, 0)
    return pl.pallas_call(
        kern,
        grid=(m // tb,),
        in_specs=[pl.BlockSpec((tb, 2 * D_FF), lambda i: (i, 0)),
                  pl.BlockSpec((SUBLANES, 2 * D_FF), lambda i: (jnp.maximum(i * per - 1, 0), 0)),
                  pl.BlockSpec((SUBLANES, 2 * D_FF), lambda i: (jnp.minimum((i + 1) * per, nrow8 - 1), 0)),
                  pl.BlockSpec((FFN_CONV, 2 * D_FF), lambda i: (0, 0)),
                  pl.BlockSpec((1, 2 * D_FF), lambda i: (0, 0)),
                  pl.BlockSpec((D_FF, d), lambda i: (0, 0)),
                  pl.BlockSpec((tb, d), lambda i: (i, 0)),
                  pl.BlockSpec((1, 12, d), lambda i: (i // nblk, 0, 0)),
                  pl.BlockSpec((1, d), lambda i: (0, 0))],
        out_specs=pl.BlockSpec((tb, d), out_map),
        out_shape=jax.ShapeDtypeStruct((out_rows, d), F32),
        compiler_params=_cparams(("arbitrary",)),
        name="ffn_down",
    )(u, u, u, cw, cb, wd, x, modsel, gp)


def _rope_tables(seq, nctx):
    rows = seq // GRID_W
    row = np.repeat(np.arange(rows), GRID_W)
    col = np.tile(np.arange(GRID_W), rows)
    axis_dim = HEAD_DIM // 2
    inv_freq = (1.0 / (np.float32(ROPE_THETA) ** (np.arange(0, axis_dim, 2, dtype=np.float32) / np.float32(axis_dim))))
    ang = np.stack([row, col], axis=-1).astype(np.float32)[:, :, None] * inv_freq.astype(np.float32)
    cos, sin = np.cos(ang), np.sin(ang)
    cos_t = np.concatenate([cos[:, 0], cos[:, 0], cos[:, 1], cos[:, 1]], axis=-1)
    sin_t = np.concatenate([-sin[:, 0], sin[:, 0], -sin[:, 1], sin[:, 1]], axis=-1)
    cos_t = np.concatenate([np.ones((nctx, HEAD_DIM), np.float32), cos_t], axis=0)
    sin_t = np.concatenate([np.zeros((nctx, HEAD_DIM), np.float32), sin_t], axis=0)
    return jnp.asarray(cos_t, F32), jnp.asarray(sin_t, F32)


def _permute_w_in(w):
    offs = [0]
    for s in IN_SPLITS:
        offs.append(offs[-1] + s)
    seg = lambda i: w[:, offs[i]:offs[i + 1]]
    gq, gk, gv, gr, lrf, lrb, aq, ak, av, lx, ly, gates = (seg(i) for i in range(12))
    pad = jnp.zeros((w.shape[0], D_IN_PAD - COL_LR - 2 * GLA_RANK), w.dtype)
    return jnp.concatenate([gq, gk, gv, gr, aq, lx, ly, gates, ak, av, lrf, lrb, pad], axis=1).astype(BF16)


def _largest_tile(t_len, cap):
    best = SUBLANES
    for cand in range(SUBLANES, cap + 1, SUBLANES):
        if t_len % cand == 0:
            best = cand
    return best


def kernel(x, c, ctx, c_ctx, w_ada, b_ada, norm_gains, w_in, gla_w_decay, gla_b_decay, gla_norm_g,
           att_q_norm_g, att_k_norm_g, lru_conv_w, lru_conv_b, lru_w_gates, lru_b_gates, lru_lambda,
           w_branch, w_out, ffn_w_up, ffn_conv_w, ffn_conv_b, ffn_w_down):
    batch, seq, d = x.shape
    nctx = ctx.shape[1]
    depth = w_ada.shape[0]
    assert d == D_MODEL and nctx == TIME_BLOCK and seq % TIME_BLOCK == 0 and seq % GRID_W == 0
    t_len = nctx + seq
    m = batch * t_len
    tm_big = _largest_tile(t_len, 1408)
    tq = _largest_tile(t_len, 768)
    assert tm_big % TIME_BLOCK == 0 or tm_big % LANES == 0
    assert tq % TIME_BLOCK == 0

    xc = jnp.concatenate([ctx, x], axis=1).reshape(m, d)

    rows = -(-(batch + 1) // SUBLANES) * SUBLANES
    cvec = jnp.zeros((rows, d), F32).at[:batch].set(c).at[batch].set(c_ctx)
    mod = _ada_call(cvec, w_ada, b_ada)
    mod = mod.reshape(depth, rows, 6, d)
    cos_t, sin_t = _rope_tables(seq, nctx)

    for l in range(depth):
        mod_c = jnp.broadcast_to(mod[l, batch][None], (batch, 6, d))
        modsel = jnp.concatenate([mod_c, mod[l, :batch]], axis=1)
        g_pre_mix, g_post_mix, g_pre_ffn, g_post_ffn = (norm_gains[l, i].reshape(1, d) for i in range(4))

        p = _nmm_call(xc, g_pre_mix, modsel, _permute_w_in(w_in[l]), shift_idx=0, scale_idx=1,
                      tm=tm_big, tn=1024, t_len=t_len, nctx=nctx, name="in_proj")

        wd_pad = jnp.zeros((2, LANES, GLA_QK), F32)
        wd_pad = wd_pad.at[0, :GLA_RANK].set(gla_w_decay[l, 0]).at[1, GLA_RANK:2 * GLA_RANK].set(gla_w_decay[l, 1])
        wd_pad = jnp.stack(_split_bf16(wd_pad), axis=1)
        bd = gla_b_decay[l].reshape(2, 1, GLA_QK)
        o_f, o_b = _gla_call(p, wd_pad, bd, batch=batch, t_len=t_len)

        qn, kn, vext = _qkprep_call(p, cos_t, sin_t, att_q_norm_g[l].reshape(1, HEAD_DIM),
                                    att_k_norm_g[l].reshape(1, HEAD_DIM), t_len=t_len)
        att = _attn_call(qn, kn, vext, batch=batch, t_len=t_len, tq=tq)
        if l < depth - 1:
            att = _attn_ctx_call(qn, kn, vext, att, batch=batch, t_len=t_len, nctx=nctx)

        wg = lru_w_gates[l].astype(BF16)
        lam = lru_lambda[l].reshape(2, 1, LRU_WIDTH)
        cb = lru_conv_b[l].reshape(1, LRU_WIDTH)
        h_f = _lru_call(p, lru_conv_w[l], cb, wg, lru_b_gates[l], lam, reverse=False, batch=batch, t_len=t_len)
        h_b = _lru_call(p, lru_conv_w[l], cb, wg, lru_b_gates[l], lam, reverse=True, batch=batch, t_len=t_len)

        xc = _merge_call(o_f, o_b, p, att, h_f, h_b, xc, modsel, gla_norm_g[l].reshape(1, GLA_V), g_post_mix,
                         w_branch[l].astype(BF16), w_out[l].astype(BF16), tm=_largest_tile(t_len, 528),
                         t_len=t_len, nctx=nctx)

        u = _nmm_call(xc, g_pre_ffn, modsel, ffn_w_up[l].astype(BF16), shift_idx=3, scale_idx=4,
                      tm=_largest_tile(t_len, 704), tn=D_FF, t_len=t_len, nctx=nctx, name="ffn_up")
        xc = _ffn_down_call(u, ffn_conv_w[l], ffn_conv_b[l].reshape(1, 2 * D_FF), ffn_w_down[l].astype(BF16),
                            xc, modsel, g_post_ffn, t_len=t_len, latent_only=(l == depth - 1))

    return xc.reshape(batch, seq, d)
```
